```python
import math
import jax, jax.numpy as jnp
from jax import lax
import numpy as np

D_MODEL = 1024
BATCH = 8
SEQ = 2048
DEPTH = 4

CHUNK = 64
N_META = 16
N_A_LAYERS = DEPTH // 2
N_B_LAYERS = DEPTH - N_A_LAYERS
D_RNN = 3 * D_MODEL // 2
N_LRU_BLOCKS = 16
LRU_BLOCK = D_RNN // N_LRU_BLOCKS
LRU_C = 8.0
CONV_A_WIDTH = 4
N_FOX_HEADS = 16
FOX_HEAD_DIM = D_MODEL // N_FOX_HEADS
Q_BLOCK = 128
D_FF = ((8 * D_MODEL // 3 + 255) // 256) * 256
CONV_F_WIDTH = 3
DN_ALPHA = (2 * DEPTH) ** 0.25
DN_BETA = (8 * DEPTH) ** -0.25
LN_EPS = 1e-5

kernel_name = "yoco_rglru_fox_convffn_deepnorm"


def layer_norm(x, g, b):
    xf = x.astype(jnp.float32)
    mu = jnp.mean(xf, axis=-1, keepdims=True)
    var = jnp.mean(jnp.square(xf - mu), axis=-1, keepdims=True)
    y = (xf - mu) * lax.rsqrt(var + LN_EPS)
    return (y * g.astype(jnp.float32) + b.astype(jnp.float32)).astype(x.dtype)


def causal_dwconv(x, w, b):
    width = w.shape[0]
    length = x.shape[1]
    xp = jnp.pad(x, ((0, 0), (width - 1, 0), (0, 0)))
    y = b
    for k in range(width):
        y = y + xp[:, k:k + length] * w[k]
    return y


def rg_lru(x, w_r, b_r, w_i, b_i, lam):
    bsz, length, width = x.shape
    xb = x.reshape(bsz, length, N_LRU_BLOCKS, LRU_BLOCK)
    r = jax.nn.sigmoid(jnp.einsum('blnc,ncd->blnd', xb, w_r).reshape(bsz, length, width) + b_r)
    i = jax.nn.sigmoid(jnp.einsum('blnc,ncd->blnd', xb, w_i).reshape(bsz, length, width) + b_i)
    log_a = -LRU_C * r.astype(jnp.float32) * jax.nn.softplus(-lam.astype(jnp.float32))
    a = jnp.exp(log_a)
    u = jnp.sqrt(-jnp.expm1(2.0 * log_a)) * (i * x).astype(jnp.float32)

    def combine(left, right):
        a1, b1 = left
        a2, b2 = right
        return a1 * a2, a2 * b1 + b2

    _, h = lax.associative_scan(combine, (a, u), axis=1)
    return h.astype(x.dtype)


def recurrent_mixer(x, w_in, conv_w, conv_b, w_r, b_r, w_i, b_i, lam, w_out):
    gr = x @ w_in
    gate, rec = gr[..., :D_RNN], gr[..., D_RNN:]
    rec = causal_dwconv(rec, conv_w, conv_b)
    h = rg_lru(rec, w_r, b_r, w_i, b_i, lam)
    return (jax.nn.gelu(gate) * h) @ w_out


def conv_ffn(x, w_in, conv_w, conv_b, w_out):
    h = causal_dwconv(x @ w_in, conv_w, conv_b)
    gate, val = h[..., :D_FF], h[..., D_FF:]
    return (jax.nn.gelu(gate) * val) @ w_out


def to_heads_padded(t, lp):
    bsz, length, _ = t.shape
    t = t.reshape(bsz, length, N_FOX_HEADS, FOX_HEAD_DIM).transpose(0, 2, 1, 3)
    return jnp.pad(t, ((0, 0), (0, 0), (0, lp - length), (0, 0)))


def shared_kv(x, kv_w, f_b):
    length = x.shape[1]
    lp = -(-length // Q_BLOCK) * Q_BLOCK
    z = x @ kv_w
    k = to_heads_padded(z[..., :D_MODEL], lp)
    v = to_heads_padded(z[..., D_MODEL:2 * D_MODEL], lp)
    log_f = jax.nn.log_sigmoid(z[..., 2 * D_MODEL:].astype(jnp.float32) + f_b.astype(jnp.float32))
    c = jnp.cumsum(log_f, axis=1).transpose(0, 2, 1)
    c = jnp.pad(c, ((0, 0), (0, 0), (0, lp - length)), mode='edge')
    return k, v, c


def forgetting_attention(q, k, v, c):
    lp = q.shape[2]
    scale = q.shape[-1] ** -0.5
    outs = []
    for q0 in range(0, lp, Q_BLOCK):
        end = q0 + Q_BLOCK
        s = jnp.einsum('bhqd,bhkd->bhqk', q[:, :, q0:end], k[:, :, :end]).astype(jnp.float32) * scale
        s = s + c[:, :, q0:end, None] - c[:, :, None, :end]
        mask = jnp.arange(end)[None, :] <= jnp.arange(q0, end)[:, None]
        s = jnp.where(mask, s, -jnp.inf)
        p = jax.nn.softmax(s, axis=-1).astype(v.dtype)
        outs.append(jnp.einsum('bhqk,bhkd->bhqd', p, v[:, :, :end]))
    return jnp.concatenate(outs, axis=2)


def fox_mixer(x, w_in, w_out, k, v, c):
    bsz, length, _ = x.shape
    qg = x @ w_in
    q = to_heads_padded(qg[..., :D_MODEL], k.shape[2])
    o = forgetting_attention(q, k, v, c)[:, :, :length]
    o = o.transpose(0, 2, 1, 3).reshape(bsz, length, D_MODEL)
    return (o * jax.nn.sigmoid(qg[..., D_MODEL:])) @ w_out


def setup_inputs(seed: int = 0) -> dict:
    key = jax.random.key(seed)
    ks = jax.random.split(key, 24)
    f32 = jnp.float32
    d = D_MODEL

    def nrm(k, shape, scale):
        return jax.random.normal(k, shape, f32) * scale

    u = jax.random.uniform(ks[9], (N_A_LAYERS, D_RNN), f32, 0.9, 0.999)
    a0 = u ** (1.0 / LRU_C)
    lam = jnp.log(a0) - jnp.log1p(-a0)

    kv_w = jnp.concatenate([
        nrm(ks[11], (d, d), d ** -0.5),
        nrm(ks[12], (d, d), d ** -0.5 * DN_BETA),
        nrm(ks[13], (d, N_FOX_HEADS), d ** -0.5),
    ], axis=1)

    return {
        "x": nrm(ks[0], (BATCH, SEQ, d), 1.0),
        "meta": nrm(ks[1], (N_META, d), 1.0),
        "a_w_in": nrm(ks[2], (N_A_LAYERS, d, 2 * D_RNN), d ** -0.5),
        "a_conv_w": nrm(ks[3], (N_A_LAYERS, CONV_A_WIDTH, D_RNN), CONV_A_WIDTH ** -0.5),
        "a_conv_b": nrm(ks[4], (N_A_LAYERS, D_RNN), 0.02),
        "a_w_r": nrm(ks[5], (N_A_LAYERS, N_LRU_BLOCKS, LRU_BLOCK, LRU_BLOCK), LRU_BLOCK ** -0.5),
        "a_b_r": nrm(ks[6], (N_A_LAYERS, D_RNN), 0.02),
        "a_w_i": nrm(ks[7], (N_A_LAYERS, N_LRU_BLOCKS, LRU_BLOCK, LRU_BLOCK), LRU_BLOCK ** -0.5),
        "a_b_i": nrm(ks[8], (N_A_LAYERS, D_RNN), 0.02),
        "a_lambda": lam,
        "a_w_out": nrm(ks[10], (N_A_LAYERS, D_RNN, d), D_RNN ** -0.5 * DN_BETA),
        "kv_w": kv_w,
        "kv_f_b": jax.random.uniform(ks[14], (N_FOX_HEADS,), f32, 1.0, 4.0),
        "b_w_in": nrm(ks[15], (N_B_LAYERS, d, 2 * d), d ** -0.5),
        "b_w_out": nrm(ks[16], (N_B_LAYERS, d, d), d ** -0.5 * DN_BETA),
        "f_w_in": nrm(ks[17], (DEPTH, d, 2 * D_FF), d ** -0.5),
        "f_conv_w": nrm(ks[18], (DEPTH, CONV_F_WIDTH, 2 * D_FF), CONV_F_WIDTH ** -0.5),
        "f_conv_b": nrm(ks[19], (DEPTH, 2 * D_FF), 0.02),
        "f_w_out": nrm(ks[20], (DEPTH, D_FF, d), D_FF ** -0.5 * DN_BETA),
        "ln1_g": 1.0 + nrm(ks[21], (DEPTH, d), 0.02),
        "ln1_b": nrm(ks[22], (DEPTH, d), 0.02),
        "ln2_g": 1.0 + nrm(ks[23], (DEPTH, d), 0.02),
        "ln2_b": nrm(jax.random.fold_in(key, 99), (DEPTH, d), 0.02),
    }


def reference(x, meta, a_w_in, a_conv_w, a_conv_b, a_w_r, a_b_r, a_w_i, a_b_i, a_lambda, a_w_out,
              kv_w, kv_f_b, b_w_in, b_w_out, f_w_in, f_conv_w, f_conv_b, f_w_out,
              ln1_g, ln1_b, ln2_g, ln2_b):
    bsz = x.shape[0]
    h = jnp.concatenate([jnp.broadcast_to(meta.astype(x.dtype), (bsz, N_META, D_MODEL)), x], axis=1)
    k = v = c = None
    for layer in range(DEPTH):
        if layer < N_A_LAYERS:
            mix = recurrent_mixer(h, a_w_in[layer], a_conv_w[layer], a_conv_b[layer],
                                  a_w_r[layer], a_b_r[layer], a_w_i[layer], a_b_i[layer],
                                  a_lambda[layer], a_w_out[layer])
        else:
            if layer == N_A_LAYERS:
                k, v, c = shared_kv(h, kv_w, kv_f_b)
            j = layer - N_A_LAYERS
            mix = fox_mixer(h, b_w_in[j], b_w_out[j], k, v, c)
        h = layer_norm(DN_ALPHA * h + mix, ln1_g[layer], ln1_b[layer])
        ffn = conv_ffn(h, f_w_in[layer], f_conv_w[layer], f_conv_b[layer], f_w_out[layer])
        h = layer_norm(DN_ALPHA * h + ffn, ln2_g[layer], ln2_b[layer])
    return h[:, N_META:]
```

```python
import functools
import math

import jax
import jax.numpy as jnp
from jax import lax
from jax.experimental import pallas as pl
from jax.experimental.pallas import tpu as pltpu

D_MODEL = 1024
DEPTH = 4
N_META = 16
N_A_LAYERS = DEPTH // 2
D_RNN = 3 * D_MODEL // 2
N_LRU_BLOCKS = 16
LRU_BLOCK = D_RNN // N_LRU_BLOCKS
LRU_C = 8.0
CONV_A_WIDTH = 4
N_FOX_HEADS = 16
FOX_HEAD_DIM = D_MODEL // N_FOX_HEADS
D_FF = 2816
CONV_F_WIDTH = 3
DN_ALPHA = (2 * DEPTH) ** 0.25
LN_EPS = 1e-5

SUBLANES = 8
LANES = 128
ROW_TILE = 688
RNN_GROUP = 4 * LRU_BLOCK
N_RNN_GROUPS = D_RNN // RNN_GROUP
FF_CHUNK = 256
N_FF_CHUNKS = D_FF // FF_CHUNK
ATT_TILE = 256
HEADS_PER_STEP = LANES // FOX_HEAD_DIM
MASK_VALUE = -1e30
VMEM_LIMIT = 56 * 1024 * 1024

F32 = jnp.float32
BF16 = jnp.bfloat16


def _dot(a, b):
    return jnp.dot(a, b, preferred_element_type=F32)


def _layer_norm_rows(z, g, b):
    mu = jnp.mean(z, axis=-1, keepdims=True)
    zc = z - mu
    var = jnp.mean(zc * zc, axis=-1, keepdims=True)
    return zc * lax.rsqrt(var + LN_EPS) * g + b


def _gelu_tanh(x):
    c = math.sqrt(2.0 / math.pi)
    return 0.5 * x * (1.0 + jnp.tanh(c * (x + 0.044715 * (x * x * x))))


def _sigmoid(x):
    return 1.0 / (1.0 + jnp.exp(-x))


def _softplus(x):
    return jnp.maximum(x, 0.0) + jnp.log1p(jnp.exp(-jnp.abs(x)))


def _const_spec(shape):
    return pl.BlockSpec(shape, lambda *_: (0,) * len(shape), pipeline_mode=pl.Buffered(1))


def _row_spec(tl, width):
    return pl.BlockSpec((1, tl, width), lambda b, t: (b, t, 0))


def _params():
    return pltpu.CompilerParams(dimension_semantics=("arbitrary", "arbitrary"),
                                vmem_limit_bytes=VMEM_LIMIT)


def _scan_rows(a_buf, u_buf, h_buf, carry, n_rows):
    row = lax.broadcasted_iota(jnp.int32, (SUBLANES, a_buf.shape[1]), 0)

    def body(gi, c):
        start = pl.multiple_of(gi * SUBLANES, SUBLANES)
        a = a_buf[pl.ds(start, SUBLANES), :]
        u = u_buf[pl.ds(start, SUBLANES), :]
        for d in (1, 2, 4):
            keep = row >= d
            a_prev = jnp.where(keep, pltpu.roll(a, d, 0), 1.0)
            u_prev = jnp.where(keep, pltpu.roll(u, d, 0), 0.0)
            u = a * u_prev + u
            a = a * a_prev
        h = u + a * c
        h_buf[pl.ds(start, SUBLANES), :] = h
        return h[SUBLANES - 1:SUBLANES, :]

    return lax.fori_loop(0, n_rows // SUBLANES, body, carry)


def _recurrent_kernel(x_ref, w_in_ref, cw_ref, cb_ref, wg_ref, br_ref, bi_ref, lam_ref,
                      w_out_ref, g_ref, b_ref, o_ref,
                      tail_ref, hc_ref, p_buf, a_buf, u_buf, h_buf, y_buf):
    tl = x_ref.shape[1]

    @pl.when(pl.program_id(1) == 0)
    def _():
        tail_ref[...] = jnp.zeros_like(tail_ref)
        hc_ref[...] = jnp.zeros_like(hc_ref)

    x = x_ref[0]
    xb = x.astype(BF16)
    for g in range(N_RNN_GROUPS):
        c0, c1 = g * RNN_GROUP, (g + 1) * RNN_GROUP
        gate = _dot(xb, w_in_ref[:, c0:c1])
        rec = _dot(xb, w_in_ref[:, D_RNN + c0:D_RNN + c1])
        p_buf[0:SUBLANES, :] = tail_ref[:, c0:c1]
        p_buf[SUBLANES:, :] = rec
        tail_ref[:, c0:c1] = rec[tl - SUBLANES:, :]
        rc = cb_ref[:, c0:c1]
        for k in range(CONV_A_WIDTH):
            off = SUBLANES - (CONV_A_WIDTH - 1) + k
            rc = rc + p_buf[off:off + tl, :] * cw_ref[k:k + 1, c0:c1]
        ri = _dot(rc.astype(BF16), wg_ref[g])
        r = _sigmoid(ri[:, :RNN_GROUP] + br_ref[:, c0:c1])
        i = _sigmoid(ri[:, RNN_GROUP:] + bi_ref[:, c0:c1])
        log_a = (-LRU_C) * r * _softplus(-lam_ref[:, c0:c1])
        a_buf[...] = jnp.exp(log_a)
        t = -jnp.tanh(log_a)
        u_buf[...] = jnp.sqrt(2.0 * t / (1.0 + t)) * (i * rc)
        hc_ref[:, c0:c1] = _scan_rows(a_buf, u_buf, h_buf, hc_ref[:, c0:c1], tl)
        y_buf[:, c0:c1] = (_gelu_tanh(gate) * h_buf[...]).astype(BF16)
    mix = _dot(y_buf[...], w_out_ref[...])
    o_ref[0] = _layer_norm_rows(DN_ALPHA * x + mix, g_ref[...], b_ref[...])


def _recurrent_layer(h, w_in, cw, cb, wg, br, bi, lam, w_out, ln_g, ln_b):
    bsz, length, d = h.shape
    tl = ROW_TILE
    return pl.pallas_call(
        _recurrent_kernel,
        out_shape=jax.ShapeDtypeStruct(h.shape, F32),
        grid=(bsz, length // tl),
        in_specs=[_row_spec(tl, d), _const_spec(w_in.shape), _const_spec(cw.shape),
                  _const_spec(cb.shape), _const_spec(wg.shape), _const_spec(br.shape),
                  _const_spec(bi.shape), _const_spec(lam.shape), _const_spec(w_out.shape),
                  _const_spec(ln_g.shape), _const_spec(ln_b.shape)],
        out_specs=_row_spec(tl, d),
        scratch_shapes=[pltpu.VMEM((SUBLANES, D_RNN), F32),
                        pltpu.VMEM((1, D_RNN), F32),
                        pltpu.VMEM((tl + SUBLANES, RNN_GROUP), F32),
                        pltpu.VMEM((tl, RNN_GROUP), F32),
                        pltpu.VMEM((tl, RNN_GROUP), F32),
                        pltpu.VMEM((tl, RNN_GROUP), F32),
                        pltpu.VMEM((tl, D_RNN), BF16)],
        compiler_params=_params(),
        name="recurrent_mixer",
    )(h, w_in, cw, cb, wg, br, bi, lam, w_out, ln_g, ln_b)


def _ffn_kernel(x_ref, w_in_ref, cw_ref, cb_ref, w_out_ref, g_ref, b_ref, o_ref,
                tail_ref, p_buf, y_buf):
    tl = x_ref.shape[1]

    @pl.when(pl.program_id(1) == 0)
    def _():
        tail_ref[...] = jnp.zeros_like(tail_ref)

    x = x_ref[0]
    xb = x.astype(BF16)

    def conv_branch(c0, c1):
        hid = _dot(xb, w_in_ref[:, c0:c1])
        p_buf[0:SUBLANES, :] = tail_ref[:, c0:c1]
        p_buf[SUBLANES:, :] = hid
        tail_ref[:, c0:c1] = hid[tl - SUBLANES:, :]
        out = cb_ref[:, c0:c1]
        for k in range(CONV_F_WIDTH):
            off = SUBLANES - (CONV_F_WIDTH - 1) + k
            out = out + p_buf[off:off + tl, :] * cw_ref[k:k + 1, c0:c1]
        return out

    for c in range(N_FF_CHUNKS):
        c0, c1 = c * FF_CHUNK, (c + 1) * FF_CHUNK
        gate = conv_branch(c0, c1)
        val = conv_branch(D_FF + c0, D_FF + c1)
        y_buf[:, c0:c1] = (_gelu_tanh(gate) * val).astype(BF16)
    ffn = _dot(y_buf[...], w_out_ref[...])
    o_ref[0] = _layer_norm_rows(DN_ALPHA * x + ffn, g_ref[...], b_ref[...])


def _ffn_layer(h, w_in, cw, cb, w_out, ln_g, ln_b):
    bsz, length, d = h.shape
    tl = ROW_TILE
    return pl.pallas_call(
        _ffn_kernel,
        out_shape=jax.ShapeDtypeStruct(h.shape, F32),
        grid=(bsz, length // tl),
        in_specs=[_row_spec(tl, d), _const_spec(w_in.shape), _const_spec(cw.shape),
                  _const_spec(cb.shape), _const_spec(w_out.shape),
                  _const_spec(ln_g.shape), _const_spec(ln_b.shape)],
        out_specs=_row_spec(tl, d),
        scratch_shapes=[pltpu.VMEM((SUBLANES, 2 * D_FF), F32),
                        pltpu.VMEM((tl + SUBLANES, FF_CHUNK), F32),
                        pltpu.VMEM((tl, D_FF), BF16)],
        compiler_params=_params(),
        name="conv_ffn",
    )(h, w_in, cw, cb, w_out, ln_g, ln_b)


def _kv_kernel(x_ref, wk_ref, wv_ref, wf_ref, fb_ref, k_ref, v_ref, c_ref, carry_ref):
    tl = x_ref.shape[1]

    @pl.when(pl.program_id(1) == 0)
    def _():
        carry_ref[...] = jnp.zeros_like(carry_ref)

    xb = x_ref[0].astype(BF16)
    k_ref[0] = _dot(xb, wk_ref[...]).astype(BF16)
    v_ref[0] = _dot(xb, wv_ref[...]).astype(BF16)
    zf = _dot(xb, wf_ref[...]) + fb_ref[...]
    c = jnp.minimum(zf, 0.0) - jnp.log1p(jnp.exp(-jnp.abs(zf)))
    row = lax.broadcasted_iota(jnp.int32, c.shape, 0)
    d = 1
    while d < tl:
        c = c + jnp.where(row >= d, pltpu.roll(c, d, 0), 0.0)
        d *= 2
    c = c + carry_ref[...]
    carry_ref[...] = c[tl - 1:tl, :]
    c_ref[0] = c


def _kv_projection(h, wk, wv, wf, fb):
    bsz, length, d = h.shape
    tl = ROW_TILE
    return pl.pallas_call(
        _kv_kernel,
        out_shape=(jax.ShapeDtypeStruct(h.shape, BF16), jax.ShapeDtypeStruct(h.shape, BF16),
                   jax.ShapeDtypeStruct((bsz, length, LANES), F32)),
        grid=(bsz, length // tl),
        in_specs=[_row_spec(tl, d), _const_spec(wk.shape), _const_spec(wv.shape),
                  _const_spec(wf.shape), _const_spec(fb.shape)],
        out_specs=(_row_spec(tl, d), _row_spec(tl, d), _row_spec(tl, LANES)),
        scratch_shapes=[pltpu.VMEM((1, LANES), F32)],
        compiler_params=_params(),
        name="kv_projection",
    )(h, wk, wv, wf, fb)


def _q_kernel(x_ref, w_ref, q_ref, gate_ref):
    xb = x_ref[0].astype(BF16)
    scale = FOX_HEAD_DIM ** -0.5
    q_ref[0] = (_dot(xb, w_ref[:, :D_MODEL]) * scale).astype(BF16)
    gate_ref[0] = _sigmoid(_dot(xb, w_ref[:, D_MODEL:]))


def _q_projection(h, w):
    bsz, length, d = h.shape
    tl = ROW_TILE
    return pl.pallas_call(
        _q_kernel,
        out_shape=(jax.ShapeDtypeStruct(h.shape, BF16), jax.ShapeDtypeStruct(h.shape, F32)),
        grid=(bsz, length // tl),
        in_specs=[_row_spec(tl, d), _const_spec(w.shape)],
        out_specs=(_row_spec(tl, d), _row_spec(tl, d)),
        compiler_params=_params(),
        name="q_projection",
    )(h, w)


def _attention_kernel(q_ref, k_ref, v_ref, c_ref, o_ref):
    length = q_ref.shape[1]
    n_full_tiles = length // ATT_TILE
    rem = length - n_full_tiles * ATT_TILE
    lane = lax.broadcasted_iota(jnp.int32, (1, LANES), 1)
    head_lanes = [(lane >= hh * FOX_HEAD_DIM) & (lane < (hh + 1) * FOX_HEAD_DIM)
                  for hh in range(HEADS_PER_STEP)]

    def block_update(state, qh, hh, k_blk, v_blk, c_row, mask):
        m, l, acc = state
        s = lax.dot_general(qh, k_blk, (((1,), (1,)), ((), ())),
                            preferred_element_type=F32) - c_row
        if mask is not None:
            s = jnp.where(mask, s, MASK_VALUE)
        m_new = jnp.maximum(m, jnp.max(s, axis=1, keepdims=True))
        alpha = jnp.exp(m - m_new)
        p = jnp.exp(s - m_new)
        l = alpha * l + jnp.sum(p, axis=1, keepdims=True)
        acc = alpha * acc + _dot(p.astype(BF16), v_blk)
        return m_new, l, acc

    def attend(q_start, tq, n_full, diag_start, diag_len):
        q = q_ref[0, pl.ds(q_start, tq), :]
        qs = [jnp.where(head_lanes[hh], q, jnp.zeros_like(q)) for hh in range(HEADS_PER_STEP)]
        init = tuple((jnp.full((tq, 1), MASK_VALUE, F32), jnp.zeros((tq, 1), F32),
                      jnp.zeros((tq, LANES), F32)) for _ in range(HEADS_PER_STEP))

        def body(j, states):
            start = pl.multiple_of(j * ATT_TILE, ATT_TILE)
            k_blk = k_ref[0, pl.ds(start, ATT_TILE), :]
            v_blk = v_ref[0, pl.ds(start, ATT_TILE), :]
            return tuple(block_update(states[hh], qs[hh], hh, k_blk, v_blk, c_ref[0, hh, j], None)
                         for hh in range(HEADS_PER_STEP))

        states = lax.fori_loop(0, n_full, body, init)
        k_blk = k_ref[0, pl.ds(diag_start, diag_len), :]
        v_blk = v_ref[0, pl.ds(diag_start, diag_len), :]
        causal = (lax.broadcasted_iota(jnp.int32, (tq, diag_len), 1)
                  <= lax.broadcasted_iota(jnp.int32, (tq, diag_len), 0))
        outs = []
        for hh in range(HEADS_PER_STEP):
            c_row = c_ref[0, hh, n_full][:, :diag_len]
            m, l, acc = block_update(states[hh], qs[hh], hh, k_blk, v_blk, c_row, causal)
            outs.append(acc / l)
        o = outs[0]
        for hh in range(1, HEADS_PER_STEP):
            o = jnp.where(head_lanes[hh], outs[hh], o)
        o_ref[0, pl.ds(q_start, tq), :] = o

    def tile_body(qi, carry):
        start = pl.multiple_of(qi * ATT_TILE, ATT_TILE)
        attend(start, ATT_TILE, qi, start, ATT_TILE)
        return carry

    lax.fori_loop(0, n_full_tiles, tile_body, 0)
    if rem:
        attend(n_full_tiles * ATT_TILE, rem, n_full_tiles, n_full_tiles * ATT_TILE, rem)


def _attention(q, k, v, c_rows):
    bsz, length, d = q.shape
    n_blk = c_rows.shape[2]
    col_spec = pl.BlockSpec((1, length, LANES), lambda b, hp: (b, 0, hp))
    return pl.pallas_call(
        _attention_kernel,
        out_shape=jax.ShapeDtypeStruct(q.shape, F32),
        grid=(bsz, N_FOX_HEADS // HEADS_PER_STEP),
        in_specs=[col_spec, col_spec, col_spec,
                  pl.BlockSpec((1, HEADS_PER_STEP, n_blk, 1, ATT_TILE),
                               lambda b, hp: (b, hp, 0, 0, 0))],
        out_specs=col_spec,
        compiler_params=_params(),
        name="fox_attention",
    )(q, k, v, c_rows)


def _out_kernel(x_ref, o_ref, gate_ref, w_ref, g_ref, b_ref, y_ref):
    mix = _dot((o_ref[0] * gate_ref[0]).astype(BF16), w_ref[...])
    y_ref[0] = _layer_norm_rows(DN_ALPHA * x_ref[0] + mix, g_ref[...], b_ref[...])


def _out_projection(h, o, gate, w, ln_g, ln_b):
    bsz, length, d = h.shape
    tl = ROW_TILE
    return pl.pallas_call(
        _out_kernel,
        out_shape=jax.ShapeDtypeStruct(h.shape, F32),
        grid=(bsz, length // tl),
        in_specs=[_row_spec(tl, d), _row_spec(tl, d), _row_spec(tl, d), _const_spec(w.shape),
                  _const_spec(ln_g.shape), _const_spec(ln_b.shape)],
        out_specs=_row_spec(tl, d),
        compiler_params=_params(),
        name="attn_out_projection",
    )(h, o, gate, w, ln_g, ln_b)


def _gate_weights(w_r, w_i):
    per = RNN_GROUP // LRU_BLOCK
    eye = jnp.eye(per, dtype=w_r.dtype)

    def block_diag(w):
        w = w.reshape(N_RNN_GROUPS, per, LRU_BLOCK, LRU_BLOCK)
        return jnp.einsum('gacd,ab->gacbd', w, eye).reshape(N_RNN_GROUPS, RNN_GROUP, RNN_GROUP)

    return jnp.concatenate([block_diag(w_r), block_diag(w_i)], axis=-1).astype(BF16)


def _row(v):
    return v.reshape(1, -1).astype(F32)


def kernel(x, meta, a_w_in, a_conv_w, a_conv_b, a_w_r, a_b_r, a_w_i, a_b_i, a_lambda, a_w_out, kv_w, kv_f_b, b_w_in, b_w_out, f_w_in, f_conv_w, f_conv_b, f_w_out, ln1_g, ln1_b, ln2_g, ln2_b):
    bsz, seq, d = x.shape
    length = seq + N_META
    assert d == D_MODEL and length % ROW_TILE == 0
    h = jnp.concatenate([jnp.broadcast_to(meta.astype(x.dtype), (bsz, N_META, d)), x], axis=1)

    k = v = c_rows = None
    for layer in range(DEPTH):
        if layer < N_A_LAYERS:
            h = _recurrent_layer(
                h, a_w_in[layer].astype(BF16), a_conv_w[layer], _row(a_conv_b[layer]),
                _gate_weights(a_w_r[layer], a_w_i[layer]), _row(a_b_r[layer]), _row(a_b_i[layer]),
                _row(a_lambda[layer]), a_w_out[layer].astype(BF16),
                _row(ln1_g[layer]), _row(ln1_b[layer]))
        else:
            if layer == N_A_LAYERS:
                wf = jnp.pad(kv_w[:, 2 * d:], ((0, 0), (0, LANES - N_FOX_HEADS))).astype(BF16)
                fb = jnp.pad(kv_f_b, (0, LANES - N_FOX_HEADS)).reshape(1, LANES).astype(F32)
                k, v, c_cols = _kv_projection(h, kv_w[:, :d].astype(BF16),
                                              kv_w[:, d:2 * d].astype(BF16), wf, fb)
                n_blk = -(-length // ATT_TILE)
                c_rows = jnp.transpose(c_cols[:, :, :N_FOX_HEADS], (0, 2, 1))
                c_rows = jnp.pad(c_rows, ((0, 0), (0, 0), (0, n_blk * ATT_TILE - length)))
                c_rows = c_rows.reshape(bsz, N_FOX_HEADS, n_blk, 1, ATT_TILE)
            j = layer - N_A_LAYERS
            q, gate = _q_projection(h, b_w_in[j].astype(BF16))
            o = _attention(q, k, v, c_rows)
            h = _out_projection(h, o, gate, b_w_out[j].astype(BF16),
                                _row(ln1_g[layer]), _row(ln1_b[layer]))
        h = _ffn_layer(h, f_w_in[layer].astype(BF16), f_conv_w[layer], _row(f_conv_b[layer]),
                       f_w_out[layer].astype(BF16), _row(ln2_g[layer]), _row(ln2_b[layer]))
    return h[:, N_META:]
```

```python
import functools
import math

import jax
import jax.numpy as jnp
from jax import lax
from jax.experimental import pallas as pl
from jax.experimental.pallas import tpu as pltpu

D_MODEL = 1024
DEPTH = 4
N_META = 16
N_A_LAYERS = DEPTH // 2
D_RNN = 3 * D_MODEL // 2
N_LRU_BLOCKS = 16
LRU_BLOCK = D_RNN // N_LRU_BLOCKS
LRU_C = 8.0
CONV_A_WIDTH = 4
N_FOX_HEADS = 16
FOX_HEAD_DIM = D_MODEL // N_FOX_HEADS
D_FF = 2816
CONV_F_WIDTH = 3
DN_ALPHA = (2 * DEPTH) ** 0.25
LN_EPS = 1e-5

SUBLANES = 8
LANES = 128
ROW_TILE = 688
RNN_GROUP = 4 * LRU_BLOCK
N_RNN_GROUPS = D_RNN // RNN_GROUP
FF_CHUNK = 256
N_FF_CHUNKS = D_FF // FF_CHUNK
ATT_TILE = 256
HEADS_PER_STEP = LANES // FOX_HEAD_DIM
MASK_VALUE = -1e30
VMEM_LIMIT = 56 * 1024 * 1024

F32 = jnp.float32
BF16 = jnp.bfloat16


def _dot(a, b):
    return jnp.dot(a, b, preferred_element_type=F32)


def _layer_norm_rows(z, g, b):
    mu = jnp.mean(z, axis=-1, keepdims=True)
    zc = z - mu
    var = jnp.mean(zc * zc, axis=-1, keepdims=True)
    return zc * lax.rsqrt(var + LN_EPS) * g + b


def _gelu_tanh(x):
    c = math.sqrt(2.0 / math.pi)
    return 0.5 * x * (1.0 + jnp.tanh(c * (x + 0.044715 * (x * x * x))))


def _sigmoid(x):
    return 1.0 / (1.0 + jnp.exp(-x))


def _softplus(x):
    return jnp.maximum(x, 0.0) + jnp.log1p(jnp.exp(-jnp.abs(x)))


def _const_spec(shape):
    return pl.BlockSpec(shape, lambda *_: (0,) * len(shape), pipeline_mode=pl.Buffered(1))


def _row_spec(tl, width):
    return pl.BlockSpec((1, tl, width), lambda b, t: (b, t, 0))


def _params():
    return pltpu.CompilerParams(dimension_semantics=("arbitrary", "arbitrary"),
                                vmem_limit_bytes=VMEM_LIMIT)


def _scan_rows(a_buf, u_buf, h_buf, carry, n_rows):
    row = lax.broadcasted_iota(jnp.int32, (SUBLANES, a_buf.shape[1]), 0)

    def body(gi, c):
        start = pl.multiple_of(gi * SUBLANES, SUBLANES)
        a = a_buf[pl.ds(start, SUBLANES), :]
        u = u_buf[pl.ds(start, SUBLANES), :]
        for d in (1, 2, 4):
            keep = row >= d
            a_prev = jnp.where(keep, pltpu.roll(a, d, 0), 1.0)
            u_prev = jnp.where(keep, pltpu.roll(u, d, 0), 0.0)
            u = a * u_prev + u
            a = a * a_prev
        h = u + a * c
        h_buf[pl.ds(start, SUBLANES), :] = h
        return h[SUBLANES - 1:SUBLANES, :]

    return lax.fori_loop(0, n_rows // SUBLANES, body, carry)


def _recurrent_kernel(x_ref, w_in_ref, cw_ref, cb_ref, wg_ref, br_ref, bi_ref, lam_ref,
                      w_out_ref, g_ref, b_ref, o_ref,
                      tail_ref, hc_ref, p_buf, a_buf, u_buf, h_buf, y_buf):
    tl = x_ref.shape[1]

    @pl.when(pl.program_id(1) == 0)
    def _():
        tail_ref[...] = jnp.zeros_like(tail_ref)
        hc_ref[...] = jnp.zeros_like(hc_ref)

    x = x_ref[0]
    xb = x.astype(BF16)
    for g in range(N_RNN_GROUPS):
        c0, c1 = g * RNN_GROUP, (g + 1) * RNN_GROUP
        gate = _dot(xb, w_in_ref[:, c0:c1])
        rec = _dot(xb, w_in_ref[:, D_RNN + c0:D_RNN + c1])
        p_buf[0:SUBLANES, :] = tail_ref[:, c0:c1]
        p_buf[SUBLANES:, :] = rec
        tail_ref[:, c0:c1] = rec[tl - SUBLANES:, :]
        rc = cb_ref[:, c0:c1]
        for k in range(CONV_A_WIDTH):
            off = SUBLANES - (CONV_A_WIDTH - 1) + k
            rc = rc + p_buf[off:off + tl, :] * cw_ref[k:k + 1, c0:c1]
        ri = _dot(rc.astype(BF16), wg_ref[g])
        r = _sigmoid(ri[:, :RNN_GROUP] + br_ref[:, c0:c1])
        i = _sigmoid(ri[:, RNN_GROUP:] + bi_ref[:, c0:c1])
        log_a = (-LRU_C) * r * _softplus(-lam_ref[:, c0:c1])
        a_buf[...] = jnp.exp(log_a)
        t = -jnp.tanh(log_a)
        u_buf[...] = jnp.sqrt(2.0 * t / (1.0 + t)) * (i * rc)
        hc_ref[:, c0:c1] = _scan_rows(a_buf, u_buf, h_buf, hc_ref[:, c0:c1], tl)
        y_buf[:, c0:c1] = (_gelu_tanh(gate) * h_buf[...]).astype(BF16)
    mix = _dot(y_buf[...], w_out_ref[...])
    o_ref[0] = _layer_norm_rows(DN_ALPHA * x + mix, g_ref[...], b_ref[...])


def _recurrent_layer(h, w_in, cw, cb, wg, br, bi, lam, w_out, ln_g, ln_b):
    bsz, length, d = h.shape
    tl = ROW_TILE
    return pl.pallas_call(
        _recurrent_kernel,
        out_shape=jax.ShapeDtypeStruct(h.shape, F32),
        grid=(bsz, length // tl),
        in_specs=[_row_spec(tl, d), _const_spec(w_in.shape), _const_spec(cw.shape),
                  _const_spec(cb.shape), _const_spec(wg.shape), _const_spec(br.shape),
                  _const_spec(bi.shape), _const_spec(lam.shape), _const_spec(w_out.shape),
                  _const_spec(ln_g.shape), _const_spec(ln_b.shape)],
        out_specs=_row_spec(tl, d),
        scratch_shapes=[pltpu.VMEM((SUBLANES, D_RNN), F32),
                        pltpu.VMEM((1, D_RNN), F32),
                        pltpu.VMEM((tl + SUBLANES, RNN_GROUP), F32),
                        pltpu.VMEM((tl, RNN_GROUP), F32),
                        pltpu.VMEM((tl, RNN_GROUP), F32),
                        pltpu.VMEM((tl, RNN_GROUP), F32),
                        pltpu.VMEM((tl, D_RNN), BF16)],
        compiler_params=_params(),
        name="recurrent_mixer",
    )(h, w_in, cw, cb, wg, br, bi, lam, w_out, ln_g, ln_b)


def _ffn_kernel(x_ref, w_in_ref, cw_ref, cb_ref, w_out_ref, g_ref, b_ref, o_ref,
                tail_ref, p_buf, y_buf):
    tl = x_ref.shape[1]

    @pl.when(pl.program_id(1) == 0)
    def _():
        tail_ref[...] = jnp.zeros_like(tail_ref)

    x = x_ref[0]
    xb = x.astype(BF16)

    def conv_branch(c0, c1):
        hid = _dot(xb, w_in_ref[:, c0:c1])
        p_buf[0:SUBLANES, :] = tail_ref[:, c0:c1]
        p_buf[SUBLANES:, :] = hid
        tail_ref[:, c0:c1] = hid[tl - SUBLANES:, :]
        out = cb_ref[:, c0:c1]
        for k in range(CONV_F_WIDTH):
            off = SUBLANES - (CONV_F_WIDTH - 1) + k
            out = out + p_buf[off:off + tl, :] * cw_ref[k:k + 1, c0:c1]
        return out

    for c in range(N_FF_CHUNKS):
        c0, c1 = c * FF_CHUNK, (c + 1) * FF_CHUNK
        gate = conv_branch(c0, c1)
        val = conv_branch(D_FF + c0, D_FF + c1)
        y_buf[:, c0:c1] = (_gelu_tanh(gate) * val).astype(BF16)
    ffn = _dot(y_buf[...], w_out_ref[...])
    o_ref[0] = _layer_norm_rows(DN_ALPHA * x + ffn, g_ref[...], b_ref[...])


def _ffn_layer(h, w_in, cw, cb, w_out, ln_g, ln_b):
    bsz, length, d = h.shape
    tl = ROW_TILE
    return pl.pallas_call(
        _ffn_kernel,
        out_shape=jax.ShapeDtypeStruct(h.shape, F32),
        grid=(bsz, length // tl),
        in_specs=[_row_spec(tl, d), _const_spec(w_in.shape), _const_spec(cw.shape),
                  _const_spec(cb.shape), _const_spec(w_out.shape),
                  _const_spec(ln_g.shape), _const_spec(ln_b.shape)],
        out_specs=_row_spec(tl, d),
        scratch_shapes=[pltpu.VMEM((SUBLANES, 2 * D_FF), F32),
                        pltpu.VMEM((tl + SUBLANES, FF_CHUNK), F32),
                        pltpu.VMEM((tl, D_FF), BF16)],
        compiler_params=_params(),
        name="conv_ffn",
    )(h, w_in, cw, cb, w_out, ln_g, ln_b)


def _kv_kernel(x_ref, wk_ref, wv_ref, wf_ref, fb_ref, k_ref, v_ref, c_ref, carry_ref):
    tl = x_ref.shape[1]

    @pl.when(pl.program_id(1) == 0)
    def _():
        carry_ref[...] = jnp.zeros_like(carry_ref)

    xb = x_ref[0].astype(BF16)
    k_ref[0] = _dot(xb, wk_ref[...]).astype(BF16)
    v_ref[0] = _dot(xb, wv_ref[...]).astype(BF16)
    zf = _dot(xb, wf_ref[...]) + fb_ref[...]
    c = jnp.minimum(zf, 0.0) - jnp.log1p(jnp.exp(-jnp.abs(zf)))
    row = lax.broadcasted_iota(jnp.int32, c.shape, 0)
    d = 1
    while d < tl:
        c = c + jnp.where(row >= d, pltpu.roll(c, d, 0), 0.0)
        d *= 2
    c = c + carry_ref[...]
    carry_ref[...] = c[tl - 1:tl, :]
    c_ref[0] = c


def _kv_projection(h, wk, wv, wf, fb):
    bsz, length, d = h.shape
    tl = ROW_TILE
    return pl.pallas_call(
        _kv_kernel,
        out_shape=(jax.ShapeDtypeStruct(h.shape, BF16), jax.ShapeDtypeStruct(h.shape, BF16),
                   jax.ShapeDtypeStruct((bsz, length, LANES), F32)),
        grid=(bsz, length // tl),
        in_specs=[_row_spec(tl, d), _const_spec(wk.shape), _const_spec(wv.shape),
                  _const_spec(wf.shape), _const_spec(fb.shape)],
        out_specs=(_row_spec(tl, d), _row_spec(tl, d), _row_spec(tl, LANES)),
        scratch_shapes=[pltpu.VMEM((1, LANES), F32)],
        compiler_params=_params(),
        name="kv_projection",
    )(h, wk, wv, wf, fb)


def _q_kernel(x_ref, w_ref, q_ref, gate_ref):
    xb = x_ref[0].astype(BF16)
    scale = FOX_HEAD_DIM ** -0.5
    q_ref[0] = (_dot(xb, w_ref[:, :D_MODEL]) * scale).astype(BF16)
    gate_ref[0] = _sigmoid(_dot(xb, w_ref[:, D_MODEL:]))


def _q_projection(h, w):
    bsz, length, d = h.shape
    tl = ROW_TILE
    return pl.pallas_call(
        _q_kernel,
        out_shape=(jax.ShapeDtypeStruct(h.shape, BF16), jax.ShapeDtypeStruct(h.shape, F32)),
        grid=(bsz, length // tl),
        in_specs=[_row_spec(tl, d), _const_spec(w.shape)],
        out_specs=(_row_spec(tl, d), _row_spec(tl, d)),
        compiler_params=_params(),
        name="q_projection",
    )(h, w)


def _nt_dot(a, b):
    return lax.dot_general(a, b, (((1,), (1,)), ((), ())), preferred_element_type=F32)


def _attention_kernel(q_ref, k_ref, v_ref, c_ref, o_ref,
                      s_buf0, s_buf1, p_buf0, p_buf1, a_buf0, a_buf1, m_buf, acc_buf, bias_buf):
    t = ATT_TILE
    n_tiles = q_ref.shape[1] // t
    n_pairs = n_tiles * (n_tiles + 1) // 2
    assert n_pairs % 2 == 0
    lane = lax.broadcasted_iota(jnp.int32, (1, LANES), 1)
    head0 = lane < FOX_HEAD_DIM
    s_bufs, p_bufs, a_bufs = (s_buf0, s_buf1), (p_buf0, p_buf1), (a_buf0, a_buf1)

    row = lax.broadcasted_iota(jnp.int32, (t, t), 0)
    col = lax.broadcasted_iota(jnp.int32, (t, t), 1)
    bias_buf[0] = jnp.zeros((t, t), F32)
    bias_buf[1] = jnp.where(col <= row, 0.0, MASK_VALUE)
    p_buf1[...] = jnp.zeros_like(p_buf1)
    a_buf1[...] = jnp.zeros_like(a_buf1)
    acc_buf[...] = jnp.zeros_like(acc_buf)
    m_buf[...] = jnp.full_like(m_buf, MASK_VALUE)

    def scores(qi, j, s_out):
        q = q_ref[0, pl.ds(pl.multiple_of(qi * t, t), t), :]
        zero = jnp.zeros_like(q)
        q2 = jnp.concatenate([jnp.where(head0, q, zero), jnp.where(head0, zero, q)], axis=0)
        s_out[...] = _nt_dot(q2, k_ref[0, pl.ds(pl.multiple_of(j * t, t), t), :])

    def softmax(qi, j, s_in, p_out, a_out):
        bias = bias_buf[(j == qi).astype(jnp.int32)]
        s = jnp.concatenate([s_in[0:t, :] - c_ref[0, 0, j] + bias,
                             s_in[t:, :] - c_ref[0, 1, j] + bias], axis=0)
        m_old = jnp.where(j == 0, MASK_VALUE, m_buf[...])
        m_new = jnp.maximum(m_old, jnp.max(s, axis=1, keepdims=True))
        m_buf[...] = m_new
        a_out[...] = jnp.exp(m_old - m_new)
        p_out[...] = jnp.exp(s - m_new).astype(BF16)

    def accumulate(j, p_in, a_in):
        v_blk = v_ref[0, pl.ds(pl.multiple_of(j * t, t), t), :]
        v_ext = jnp.concatenate([v_blk, jnp.ones((t, LANES), BF16)], axis=1)
        acc_buf[...] = acc_buf[...] * a_in[...] + _dot(p_in[...], v_ext)

    def finish_tile(qi):
        o2 = acc_buf[:, 0:LANES] / acc_buf[:, LANES:]
        o_ref[0, pl.ds(pl.multiple_of(qi * t, t), t), :] = jnp.where(head0, o2[0:t], o2[t:])

    def step(slot, carry):
        qi, j, qi_p, j_p = carry
        accumulate(j_p, p_bufs[1 - slot], a_bufs[1 - slot])

        @pl.when(j_p == qi_p)
        def _():
            finish_tile(qi_p)

        softmax(qi, j, s_bufs[slot], p_bufs[slot], a_bufs[slot])
        last = j == qi
        qi_n = jnp.minimum(jnp.where(last, qi + 1, qi), n_tiles - 1)
        j_n = jnp.where(last, 0, j + 1)
        scores(qi_n, j_n, s_bufs[1 - slot])
        return qi_n, j_n, qi, j

    def body(_, carry):
        return step(1, step(0, carry))

    zero = jnp.int32(0)
    scores(zero, zero, s_buf0)
    carry = lax.fori_loop(0, n_pairs // 2, body, (zero, zero, zero, zero + 1))
    accumulate(carry[3], p_buf1, a_buf1)
    finish_tile(n_tiles - 1)


def _attention_tail_kernel(q_ref, k_ref, v_ref, c_ref, o_in_ref, o_ref, m_buf, l_buf, acc_buf):
    del o_in_ref
    t = ATT_TILE
    rem = q_ref.shape[1]
    length = k_ref.shape[1]
    n_full = (length - rem) // t
    rows = N_FOX_HEADS * rem
    lane = lax.broadcasted_iota(jnp.int32, (1, D_MODEL), 1)
    head_lanes = [(lane >= h * FOX_HEAD_DIM) & (lane < (h + 1) * FOX_HEAD_DIM)
                  for h in range(N_FOX_HEADS)]
    q = q_ref[0]
    qs = jnp.concatenate([jnp.where(hl, q, jnp.zeros_like(q)) for hl in head_lanes], axis=0)
    m_buf[...] = jnp.full_like(m_buf, MASK_VALUE)
    l_buf[...] = jnp.zeros_like(l_buf)
    acc_buf[...] = jnp.zeros_like(acc_buf)

    def update(k_blk, v_blk, c_stack, mask):
        s = _nt_dot(qs, k_blk) - c_stack
        if mask is not None:
            s = jnp.where(mask, s, MASK_VALUE)
        m_old = m_buf[...]
        m_new = jnp.maximum(m_old, jnp.max(s, axis=1, keepdims=True))
        alpha = jnp.exp(m_old - m_new)
        p = jnp.exp(s - m_new)
        m_buf[...] = m_new
        l_buf[...] = alpha * l_buf[...] + jnp.sum(p, axis=1, keepdims=True)
        acc_buf[...] = alpha * acc_buf[...] + _dot(p.astype(BF16), v_blk)

    def c_stacked(j, width):
        return jnp.concatenate([jnp.broadcast_to(c_ref[0, h, j][:, :width], (rem, width))
                                for h in range(N_FOX_HEADS)], axis=0)

    def body(j, carry):
        start = pl.multiple_of(j * t, t)
        update(k_ref[0, pl.ds(start, t), :], v_ref[0, pl.ds(start, t), :], c_stacked(j, t), None)
        return carry

    lax.fori_loop(0, n_full, body, 0)
    r = lax.broadcasted_iota(jnp.int32, (rows, rem), 0) % rem
    cc = lax.broadcasted_iota(jnp.int32, (rows, rem), 1)
    update(k_ref[0, n_full * t:, :], v_ref[0, n_full * t:, :], c_stacked(n_full, rem), cc <= r)
    o_all = acc_buf[...] / l_buf[...]
    o = o_all[0:rem]
    for h in range(1, N_FOX_HEADS):
        o = jnp.where(head_lanes[h], o_all[h * rem:(h + 1) * rem], o)
    o_ref[0] = o


def _attention(q, k, v, c_rows):
    bsz, length, d = q.shape
    t = ATT_TILE
    n_blk = c_rows.shape[2]
    main = (length // t) * t
    rem = length - main
    col_spec = pl.BlockSpec((1, main, LANES), lambda b, hp: (b, 0, hp))
    o = pl.pallas_call(
        _attention_kernel,
        out_shape=jax.ShapeDtypeStruct(q.shape, F32),
        grid=(bsz, N_FOX_HEADS // HEADS_PER_STEP),
        in_specs=[col_spec, col_spec, col_spec,
                  pl.BlockSpec((1, HEADS_PER_STEP, n_blk, 1, t), lambda b, hp: (b, hp, 0, 0, 0))],
        out_specs=col_spec,
        scratch_shapes=[pltpu.VMEM((2 * t, t), F32), pltpu.VMEM((2 * t, t), F32),
                        pltpu.VMEM((2 * t, t), BF16), pltpu.VMEM((2 * t, t), BF16),
                        pltpu.VMEM((2 * t, 1), F32), pltpu.VMEM((2 * t, 1), F32),
                        pltpu.VMEM((2 * t, 1), F32), pltpu.VMEM((2 * t, 2 * LANES), F32),
                        pltpu.VMEM((2, t, t), F32)],
        compiler_params=_params(),
        name="fox_attention",
    )(q, k, v, c_rows)
    if not rem:
        return o
    assert main % rem == 0
    full_spec = pl.BlockSpec((1, length, d), lambda b: (b, 0, 0))
    tail_spec = pl.BlockSpec((1, rem, d), lambda b: (b, main // rem, 0))
    return pl.pallas_call(
        _attention_tail_kernel,
        out_shape=jax.ShapeDtypeStruct(q.shape, F32),
        grid=(bsz,),
        in_specs=[tail_spec, full_spec, full_spec,
                  pl.BlockSpec((1, N_FOX_HEADS, n_blk, 1, t), lambda b: (b, 0, 0, 0, 0)),
                  pl.BlockSpec(memory_space=pl.ANY)],
        out_specs=tail_spec,
        scratch_shapes=[pltpu.VMEM((N_FOX_HEADS * rem, 1), F32),
                        pltpu.VMEM((N_FOX_HEADS * rem, 1), F32),
                        pltpu.VMEM((N_FOX_HEADS * rem, d), F32)],
        input_output_aliases={4: 0},
        compiler_params=pltpu.CompilerParams(dimension_semantics=("arbitrary",),
                                             vmem_limit_bytes=VMEM_LIMIT),
        name="fox_attention_tail",
    )(q, k, v, c_rows, o)


def _out_kernel(x_ref, o_ref, gate_ref, w_ref, g_ref, b_ref, y_ref):
    mix = _dot((o_ref[0] * gate_ref[0]).astype(BF16), w_ref[...])
    y_ref[0] = _layer_norm_rows(DN_ALPHA * x_ref[0] + mix, g_ref[...], b_ref[...])


def _out_projection(h, o, gate, w, ln_g, ln_b):
    bsz, length, d = h.shape
    tl = ROW_TILE
    return pl.pallas_call(
        _out_kernel,
        out_shape=jax.ShapeDtypeStruct(h.shape, F32),
        grid=(bsz, length // tl),
        in_specs=[_row_spec(tl, d), _row_spec(tl, d), _row_spec(tl, d), _const_spec(w.shape),
                  _const_spec(ln_g.shape), _const_spec(ln_b.shape)],
        out_specs=_row_spec(tl, d),
        compiler_params=_params(),
        name="attn_out_projection",
    )(h, o, gate, w, ln_g, ln_b)


def _gate_weights(w_r, w_i):
    per = RNN_GROUP // LRU_BLOCK
    eye = jnp.eye(per, dtype=w_r.dtype)

    def block_diag(w):
        w = w.reshape(N_RNN_GROUPS, per, LRU_BLOCK, LRU_BLOCK)
        return jnp.einsum('gacd,ab->gacbd', w, eye).reshape(N_RNN_GROUPS, RNN_GROUP, RNN_GROUP)

    return jnp.concatenate([block_diag(w_r), block_diag(w_i)], axis=-1).astype(BF16)


def _row(v):
    return v.reshape(1, -1).astype(F32)


def kernel(x, meta, a_w_in, a_conv_w, a_conv_b, a_w_r, a_b_r, a_w_i, a_b_i, a_lambda, a_w_out, kv_w, kv_f_b, b_w_in, b_w_out, f_w_in, f_conv_w, f_conv_b, f_w_out, ln1_g, ln1_b, ln2_g, ln2_b):
    bsz, seq, d = x.shape
    length = seq + N_META
    assert d == D_MODEL and length % ROW_TILE == 0
    h = jnp.concatenate([jnp.broadcast_to(meta.astype(x.dtype), (bsz, N_META, d)), x], axis=1)

    k = v = c_rows = None
    for layer in range(DEPTH):
        if layer < N_A_LAYERS:
            h = _recurrent_layer(
                h, a_w_in[layer].astype(BF16), a_conv_w[layer], _row(a_conv_b[layer]),
                _gate_weights(a_w_r[layer], a_w_i[layer]), _row(a_b_r[layer]), _row(a_b_i[layer]),
                _row(a_lambda[layer]), a_w_out[layer].astype(BF16),
                _row(ln1_g[layer]), _row(ln1_b[layer]))
        else:
            if layer == N_A_LAYERS:
                wf = jnp.pad(kv_w[:, 2 * d:], ((0, 0), (0, LANES - N_FOX_HEADS))).astype(BF16)
                fb = jnp.pad(kv_f_b, (0, LANES - N_FOX_HEADS)).reshape(1, LANES).astype(F32)
                k, v, c_cols = _kv_projection(h, kv_w[:, :d].astype(BF16),
                                              kv_w[:, d:2 * d].astype(BF16), wf, fb)
                n_blk = -(-length // ATT_TILE)
                c_rows = jnp.transpose(c_cols[:, :, :N_FOX_HEADS], (0, 2, 1))
                c_rows = jnp.pad(c_rows, ((0, 0), (0, 0), (0, n_blk * ATT_TILE - length)))
                c_rows = c_rows.reshape(bsz, N_FOX_HEADS, n_blk, 1, ATT_TILE)
            j = layer - N_A_LAYERS
            q, gate = _q_projection(h, b_w_in[j].astype(BF16))
            o = _attention(q, k, v, c_rows)
            h = _out_projection(h, o, gate, b_w_out[j].astype(BF16),
                                _row(ln1_g[layer]), _row(ln1_b[layer]))
        h = _ffn_layer(h, f_w_in[layer].astype(BF16), f_conv_w[layer], _row(f_conv_b[layer]),
                       f_w_out[layer].astype(BF16), _row(ln2_g[layer]), _row(ln2_b[layer]))
    return h[:, N_META:]
```

```python
import functools
import math

import jax
import jax.numpy as jnp
from jax import lax
from jax.experimental import pallas as pl
from jax.experimental.pallas import tpu as pltpu

D_MODEL = 1024
DEPTH = 4
N_META = 16
N_A_LAYERS = DEPTH // 2
D_RNN = 3 * D_MODEL // 2
N_LRU_BLOCKS = 16
LRU_BLOCK = D_RNN // N_LRU_BLOCKS
LRU_C = 8.0
CONV_A_WIDTH = 4
N_FOX_HEADS = 16
FOX_HEAD_DIM = D_MODEL // N_FOX_HEADS
D_FF = 2816
CONV_F_WIDTH = 3
DN_ALPHA = (2 * DEPTH) ** 0.25
LN_EPS = 1e-5

SUBLANES = 8
LANES = 128
ROW_TILE = 688
RNN_GROUP = 4 * LRU_BLOCK
N_RNN_GROUPS = D_RNN // RNN_GROUP
FF_CHUNK = 256
N_FF_CHUNKS = D_FF // FF_CHUNK
ATT_TILE = 256
HEADS_PER_STEP = LANES // FOX_HEAD_DIM
MASK_VALUE = -1e30
VMEM_LIMIT = 56 * 1024 * 1024

F32 = jnp.float32
BF16 = jnp.bfloat16


def _dot(a, b):
    return jnp.dot(a, b, preferred_element_type=F32)


def _layer_norm_rows(z, g, b):
    mu = jnp.mean(z, axis=-1, keepdims=True)
    zc = z - mu
    var = jnp.mean(zc * zc, axis=-1, keepdims=True)
    return zc * lax.rsqrt(var + LN_EPS) * g + b


def _gelu_tanh(x):
    c = math.sqrt(2.0 / math.pi)
    return 0.5 * x * (1.0 + jnp.tanh(c * (x + 0.044715 * (x * x * x))))


def _sigmoid(x):
    return 1.0 / (1.0 + jnp.exp(-x))


def _softplus(x):
    return jnp.maximum(x, 0.0) + jnp.log1p(jnp.exp(-jnp.abs(x)))


def _const_spec(shape):
    return pl.BlockSpec(shape, lambda *_: (0,) * len(shape), pipeline_mode=pl.Buffered(1))


def _row_spec(tl, width):
    return pl.BlockSpec((1, tl, width), lambda b, t: (b, t, 0))


def _params():
    return pltpu.CompilerParams(dimension_semantics=("arbitrary", "arbitrary"),
                                vmem_limit_bytes=VMEM_LIMIT)


def _scan_rows(a_buf, u_buf, h_buf, carry, n_rows):
    row = lax.broadcasted_iota(jnp.int32, (SUBLANES, a_buf.shape[1]), 0)

    def body(gi, c):
        start = pl.multiple_of(gi * SUBLANES, SUBLANES)
        a = a_buf[pl.ds(start, SUBLANES), :]
        u = u_buf[pl.ds(start, SUBLANES), :]
        for d in (1, 2, 4):
            keep = row >= d
            a_prev = jnp.where(keep, pltpu.roll(a, d, 0), 1.0)
            u_prev = jnp.where(keep, pltpu.roll(u, d, 0), 0.0)
            u = a * u_prev + u
            a = a * a_prev
        h = u + a * c
        h_buf[pl.ds(start, SUBLANES), :] = h
        return h[SUBLANES - 1:SUBLANES, :]

    return lax.fori_loop(0, n_rows // SUBLANES, body, carry)


def _recurrent_kernel(x_ref, w_in_ref, cw_ref, cb_ref, wg_ref, br_ref, bi_ref, lam_ref,
                      w_out_ref, g_ref, b_ref, o_ref,
                      tail_ref, hc_ref, p_buf, a_buf, u_buf, h_buf, y_buf):
    tl = x_ref.shape[1]

    @pl.when(pl.program_id(1) == 0)
    def _():
        tail_ref[...] = jnp.zeros_like(tail_ref)
        hc_ref[...] = jnp.zeros_like(hc_ref)

    x = x_ref[0]
    xb = x.astype(BF16)
    for g in range(N_RNN_GROUPS):
        c0, c1 = g * RNN_GROUP, (g + 1) * RNN_GROUP
        gate = _dot(xb, w_in_ref[:, c0:c1])
        rec = _dot(xb, w_in_ref[:, D_RNN + c0:D_RNN + c1])
        p_buf[0:SUBLANES, :] = tail_ref[:, c0:c1]
        p_buf[SUBLANES:, :] = rec
        tail_ref[:, c0:c1] = rec[tl - SUBLANES:, :]
        rc = cb_ref[:, c0:c1]
        for k in range(CONV_A_WIDTH):
            off = SUBLANES - (CONV_A_WIDTH - 1) + k
            rc = rc + p_buf[off:off + tl, :] * cw_ref[k:k + 1, c0:c1]
        ri = _dot(rc.astype(BF16), wg_ref[g])
        r = _sigmoid(ri[:, :RNN_GROUP] + br_ref[:, c0:c1])
        i = _sigmoid(ri[:, RNN_GROUP:] + bi_ref[:, c0:c1])
        log_a = (-LRU_C) * r * _softplus(-lam_ref[:, c0:c1])
        a_buf[...] = jnp.exp(log_a)
        t = -jnp.tanh(log_a)
        u_buf[...] = jnp.sqrt(2.0 * t / (1.0 + t)) * (i * rc)
        hc_ref[:, c0:c1] = _scan_rows(a_buf, u_buf, h_buf, hc_ref[:, c0:c1], tl)
        y_buf[:, c0:c1] = (_gelu_tanh(gate) * h_buf[...]).astype(BF16)
    mix = _dot(y_buf[...], w_out_ref[...])
    o_ref[0] = _layer_norm_rows(DN_ALPHA * x + mix, g_ref[...], b_ref[...])


def _recurrent_layer(h, w_in, cw, cb, wg, br, bi, lam, w_out, ln_g, ln_b):
    bsz, length, d = h.shape
    tl = ROW_TILE
    return pl.pallas_call(
        _recurrent_kernel,
        out_shape=jax.ShapeDtypeStruct(h.shape, F32),
        grid=(bsz, length // tl),
        in_specs=[_row_spec(tl, d), _const_spec(w_in.shape), _const_spec(cw.shape),
                  _const_spec(cb.shape), _const_spec(wg.shape), _const_spec(br.shape),
                  _const_spec(bi.shape), _const_spec(lam.shape), _const_spec(w_out.shape),
                  _const_spec(ln_g.shape), _const_spec(ln_b.shape)],
        out_specs=_row_spec(tl, d),
        scratch_shapes=[pltpu.VMEM((SUBLANES, D_RNN), F32),
                        pltpu.VMEM((1, D_RNN), F32),
                        pltpu.VMEM((tl + SUBLANES, RNN_GROUP), F32),
                        pltpu.VMEM((tl, RNN_GROUP), F32),
                        pltpu.VMEM((tl, RNN_GROUP), F32),
                        pltpu.VMEM((tl, RNN_GROUP), F32),
                        pltpu.VMEM((tl, D_RNN), BF16)],
        compiler_params=_params(),
        name="recurrent_mixer",
    )(h, w_in, cw, cb, wg, br, bi, lam, w_out, ln_g, ln_b)


def _ffn_kernel(x_ref, w_in_ref, cw_ref, cb_ref, w_out_ref, g_ref, b_ref, o_ref,
                tail_ref, p_buf, y_buf):
    tl = x_ref.shape[1]

    @pl.when(pl.program_id(1) == 0)
    def _():
        tail_ref[...] = jnp.zeros_like(tail_ref)

    x = x_ref[0]
    xb = x.astype(BF16)

    def conv_branch(c0, c1):
        hid = _dot(xb, w_in_ref[:, c0:c1])
        p_buf[0:SUBLANES, :] = tail_ref[:, c0:c1]
        p_buf[SUBLANES:, :] = hid
        tail_ref[:, c0:c1] = hid[tl - SUBLANES:, :]
        out = cb_ref[:, c0:c1]
        for k in range(CONV_F_WIDTH):
            off = SUBLANES - (CONV_F_WIDTH - 1) + k
            out = out + p_buf[off:off + tl, :] * cw_ref[k:k + 1, c0:c1]
        return out

    for c in range(N_FF_CHUNKS):
        c0, c1 = c * FF_CHUNK, (c + 1) * FF_CHUNK
        gate = conv_branch(c0, c1)
        val = conv_branch(D_FF + c0, D_FF + c1)
        y_buf[:, c0:c1] = (_gelu_tanh(gate) * val).astype(BF16)
    ffn = _dot(y_buf[...], w_out_ref[...])
    o_ref[0] = _layer_norm_rows(DN_ALPHA * x + ffn, g_ref[...], b_ref[...])


def _ffn_layer(h, w_in, cw, cb, w_out, ln_g, ln_b):
    bsz, length, d = h.shape
    tl = ROW_TILE
    return pl.pallas_call(
        _ffn_kernel,
        out_shape=jax.ShapeDtypeStruct(h.shape, F32),
        grid=(bsz, length // tl),
        in_specs=[_row_spec(tl, d), _const_spec(w_in.shape), _const_spec(cw.shape),
                  _const_spec(cb.shape), _const_spec(w_out.shape),
                  _const_spec(ln_g.shape), _const_spec(ln_b.shape)],
        out_specs=_row_spec(tl, d),
        scratch_shapes=[pltpu.VMEM((SUBLANES, 2 * D_FF), F32),
                        pltpu.VMEM((tl + SUBLANES, FF_CHUNK), F32),
                        pltpu.VMEM((tl, D_FF), BF16)],
        compiler_params=_params(),
        name="conv_ffn",
    )(h, w_in, cw, cb, w_out, ln_g, ln_b)


def _kv_kernel(x_ref, wk_ref, wv_ref, wf_ref, fb_ref, k_ref, v_ref, c_ref, carry_ref):
    tl = x_ref.shape[1]

    @pl.when(pl.program_id(1) == 0)
    def _():
        carry_ref[...] = jnp.zeros_like(carry_ref)

    xb = x_ref[0].astype(BF16)
    k_ref[0] = _dot(xb, wk_ref[...]).astype(BF16)
    v_ref[0] = _dot(xb, wv_ref[...]).astype(BF16)
    zf = _dot(xb, wf_ref[...]) + fb_ref[...]
    c = jnp.minimum(zf, 0.0) - jnp.log1p(jnp.exp(-jnp.abs(zf)))
    row = lax.broadcasted_iota(jnp.int32, c.shape, 0)
    d = 1
    while d < tl:
        c = c + jnp.where(row >= d, pltpu.roll(c, d, 0), 0.0)
        d *= 2
    c = c + carry_ref[...]
    carry_ref[...] = c[tl - 1:tl, :]
    c_ref[0] = c


def _kv_projection(h, wk, wv, wf, fb):
    bsz, length, d = h.shape
    tl = ROW_TILE
    return pl.pallas_call(
        _kv_kernel,
        out_shape=(jax.ShapeDtypeStruct(h.shape, BF16), jax.ShapeDtypeStruct(h.shape, BF16),
                   jax.ShapeDtypeStruct((bsz, length, LANES), F32)),
        grid=(bsz, length // tl),
        in_specs=[_row_spec(tl, d), _const_spec(wk.shape), _const_spec(wv.shape),
                  _const_spec(wf.shape), _const_spec(fb.shape)],
        out_specs=(_row_spec(tl, d), _row_spec(tl, d), _row_spec(tl, LANES)),
        scratch_shapes=[pltpu.VMEM((1, LANES), F32)],
        compiler_params=_params(),
        name="kv_projection",
    )(h, wk, wv, wf, fb)


def _q_kernel(x_ref, w_ref, q_ref, gate_ref):
    xb = x_ref[0].astype(BF16)
    scale = FOX_HEAD_DIM ** -0.5
    q_ref[0] = (_dot(xb, w_ref[:, :D_MODEL]) * scale).astype(BF16)
    gate_ref[0] = _sigmoid(_dot(xb, w_ref[:, D_MODEL:]))


def _q_projection(h, w):
    bsz, length, d = h.shape
    tl = ROW_TILE
    return pl.pallas_call(
        _q_kernel,
        out_shape=(jax.ShapeDtypeStruct(h.shape, BF16), jax.ShapeDtypeStruct(h.shape, F32)),
        grid=(bsz, length // tl),
        in_specs=[_row_spec(tl, d), _const_spec(w.shape)],
        out_specs=(_row_spec(tl, d), _row_spec(tl, d)),
        compiler_params=_params(),
        name="q_projection",
    )(h, w)


def _nt_dot(a, b):
    return lax.dot_general(a, b, (((1,), (1,)), ((), ())), preferred_element_type=F32)


def _attention_kernel(q_ref, k_ref, v_ref, c_ref, o_ref,
                      s_buf0, s_buf1, p_buf0, p_buf1, a_buf0, a_buf1, m_buf, acc_buf, bias_buf,
                      crep_buf, vt_buf):
    t = ATT_TILE
    n_tiles = (q_ref.shape[1] // t)
    n_pairs = n_tiles * (n_tiles + 1) // 2
    assert n_pairs % 2 == 0
    ones_rows = vt_buf.shape[1] - LANES
    lane = lax.broadcasted_iota(jnp.int32, (1, LANES), 1)
    head0 = lane < FOX_HEAD_DIM

    key =lax.broadcasted_iota(jnp.int32, (t, t), 0)
    qry = lax.broadcasted_iota(jnp.int32, (t, t), 1)
    bias_buf[0] = jnp.zeros((t, t), F32)
    bias_buf[1] = jnp.where(key <= qry, 0.0, MASK_VALUE)
    for jb in range(n_tiles):
        rows = slice(jb * t, (jb + 1) * t)
        vt_buf[jb, 0:LANES, :] = v_ref[0, rows, :].astype(F32).T.astype(BF16)
        vt_buf[jb, LANES:, :] = jnp.ones((ones_rows, t), BF16)
        for hh in range(HEADS_PER_STEP):
            crep_buf[hh, rows, :] = jnp.broadcast_to(c_ref[0, 0, rows, hh:hh + 1], (t, LANES))
    o_ref[0, n_tiles * t:, :] = jnp.zeros((o_ref.shape[1] - n_tiles * t, LANES), F32)
    for buf in (p_buf0, p_buf1, a_buf0, a_buf1, acc_buf):
        buf[...] = jnp.zeros_like(buf)
    m_buf[...] = jnp.full_like(m_buf, MASK_VALUE)

    def scores(qi, j, s_out):
        q = q_ref[0, pl.ds(pl.multiple_of(qi * t, t), t), :]
        zero = jnp.zeros_like(q)
        q2 = jnp.concatenate([jnp.where(head0, q, zero), jnp.where(head0, zero, q)], axis=0)
        s_out[...] = _nt_dot(k_ref[0, pl.ds(pl.multiple_of(j * t, t), t), :], q2)

    def softmax(qi, j, s_in, p_out, a_out):
        masked = (j == qi).astype(jnp.int32)
        keys = pl.ds(pl.multiple_of(j * t, t), t)
        first = j == 0
        for hh in range(HEADS_PER_STEP):
            for half in range(t // LANES):
                cols = slice(hh * t + half * LANES, hh * t + (half + 1) * LANES)
                s = (s_in[:, cols] - crep_buf[hh, keys, :]
                     + bias_buf[masked, :, half * LANES:(half + 1) * LANES])
                m_old = jnp.where(first, MASK_VALUE, m_buf[:, cols])
                m_new = jnp.maximum(m_old, jnp.max(s, axis=0, keepdims=True))
                m_buf[:, cols] = m_new
                a_out[:, cols] = jnp.exp(m_old - m_new)
                p_out[:, cols] = jnp.exp(s - m_new).astype(BF16)

    def accumulate(qi, j, p_in, a_in):
        acc_buf[qi] = acc_buf[qi] * a_in[...] + _dot(vt_buf[j], p_in[...])

    def next_pair(pair):
        qi, j = pair
        last = j == qi
        return jnp.minimum(jnp.where(last, qi + 1, qi), n_tiles - 1), jnp.where(last, 0, j + 1)

    def body(_, carry):
        done0, done1, cur0, cur1, nxt0 = carry
        nxt1 = next_pair(nxt0)
        accumulate(*done0, p_buf0, a_buf0)
        accumulate(*done1, p_buf1, a_buf1)
        softmax(*cur0, s_buf0, p_buf0, a_buf0)
        softmax(*cur1, s_buf1, p_buf1, a_buf1)
        scores(*nxt0, s_buf0)
        scores(*nxt1, s_buf1)
        return cur0, cur1, nxt0, nxt1, next_pair(nxt1)

    zero = jnp.int32(0)
    empty = (zero, zero + 1)
    pairs = [(zero, zero)]
    for _ in range(2):
        pairs.append(next_pair(pairs[-1]))
    scores(*pairs[0], s_buf0)
    scores(*pairs[1], s_buf1)
    carry = lax.fori_loop(0, n_pairs // 2, body, (empty, empty, *pairs))
    accumulate(*carry[0], p_buf0, a_buf0)
    accumulate(*carry[1], p_buf1, a_buf1)

    head0_rows = lax.broadcasted_iota(jnp.int32, (LANES, 1), 0) < FOX_HEAD_DIM
    for qi in range(n_tiles):
        o2 = acc_buf[qi, 0:LANES, :] / acc_buf[qi, LANES:LANES + 1, :]
        o_ref[0, qi * t:(qi + 1) * t, :] = jnp.where(head0_rows, o2[:, 0:t], o2[:, t:]).T


def _attention_tail_kernel(q_ref, k_ref, v_ref, c_ref, o_in_ref, o_ref, m_buf, l_buf, acc_buf):
    del o_in_ref
    t = ATT_TILE
    rem = q_ref.shape[1]
    length = k_ref.shape[1]
    n_full = (length - rem) // t
    rows = N_FOX_HEADS * rem
    lane = lax.broadcasted_iota(jnp.int32, (1, D_MODEL), 1)
    head_lanes = [(lane >= h * FOX_HEAD_DIM) & (lane < (h + 1) * FOX_HEAD_DIM)
                  for h in range(N_FOX_HEADS)]
    q = q_ref[0]
    qs = jnp.concatenate([jnp.where(hl, q, jnp.zeros_like(q)) for hl in head_lanes], axis=0)
    m_buf[...] = jnp.full_like(m_buf, MASK_VALUE)
    l_buf[...] = jnp.zeros_like(l_buf)
    acc_buf[...] = jnp.zeros_like(acc_buf)

    def update(k_blk, v_blk, c_stack, mask):
        s = _nt_dot(qs, k_blk) - c_stack
        if mask is not None:
            s = jnp.where(mask, s, MASK_VALUE)
        m_old = m_buf[...]
        m_new = jnp.maximum(m_old, jnp.max(s, axis=1, keepdims=True))
        alpha = jnp.exp(m_old - m_new)
        p = jnp.exp(s - m_new)
        m_buf[...] = m_new
        l_buf[...] = alpha * l_buf[...] + jnp.sum(p, axis=1, keepdims=True)
        acc_buf[...] = alpha * acc_buf[...] + _dot(p.astype(BF16), v_blk)

    def c_stacked(j, width):
        return jnp.concatenate([jnp.broadcast_to(c_ref[0, h, j][:, :width], (rem, width))
                                for h in range(N_FOX_HEADS)], axis=0)

    def body(j, carry):
        start = pl.multiple_of(j * t, t)
        update(k_ref[0, pl.ds(start, t), :], v_ref[0, pl.ds(start, t), :], c_stacked(j, t), None)
        return carry

    lax.fori_loop(0, n_full, body, 0)
    r = lax.broadcasted_iota(jnp.int32, (rows, rem), 0) % rem
    cc = lax.broadcasted_iota(jnp.int32, (rows, rem), 1)
    update(k_ref[0, n_full * t:, :], v_ref[0, n_full * t:, :], c_stacked(n_full, rem), cc <= r)
    o_all = acc_buf[...] / l_buf[...]
    o = o_all[0:rem]
    for h in range(1, N_FOX_HEADS):
        o = jnp.where(head_lanes[h], o_all[h * rem:(h + 1) * rem], o)
    o_ref[0] = o


def _attention(q, k, v, c_rows, c_pairs):
    bsz, length, d = q.shape
    t = ATT_TILE
    n_blk = c_rows.shape[2]
    main = (length // t) * t
    rem = length - main
    qcols = HEADS_PER_STEP * t
    col_spec = pl.BlockSpec((1, length, LANES), lambda b, hp: (b, 0, hp))
    o = pl.pallas_call(
        _attention_kernel,
        out_shape=jax.ShapeDtypeStruct(q.shape, F32),
        grid=(bsz, N_FOX_HEADS // HEADS_PER_STEP),
        in_specs=[col_spec, col_spec, col_spec,
                  pl.BlockSpec((1, 1, length, HEADS_PER_STEP), lambda b, hp: (b, hp, 0, 0))],
        out_specs=col_spec,
        scratch_shapes=[pltpu.VMEM((t, qcols), F32), pltpu.VMEM((t, qcols), F32),
                        pltpu.VMEM((t, qcols), BF16), pltpu.VMEM((t, qcols), BF16),
                        pltpu.VMEM((1, qcols), F32), pltpu.VMEM((1, qcols), F32),
                        pltpu.VMEM((1, qcols), F32),
                        pltpu.VMEM((main // t, LANES + 2 * SUBLANES, qcols), F32),
                        pltpu.VMEM((2, t, t), F32),
                        pltpu.VMEM((HEADS_PER_STEP, main, LANES), F32),
                        pltpu.VMEM((main // t, LANES + 2 * SUBLANES, t), BF16)],
        compiler_params=_params(),
        name="fox_attention",
    )(q, k, v, c_pairs)
    if not rem:
        return o
    assert main % rem == 0
    full_spec = pl.BlockSpec((1, length, d), lambda b: (b, 0, 0))
    tail_spec = pl.BlockSpec((1, rem, d), lambda b: (b, main // rem, 0))
    return pl.pallas_call(
        _attention_tail_kernel,
        out_shape=jax.ShapeDtypeStruct(q.shape, F32),
        grid=(bsz,),
        in_specs=[tail_spec, full_spec, full_spec,
                  pl.BlockSpec((1, N_FOX_HEADS, n_blk, 1, t), lambda b: (b, 0, 0, 0, 0)),
                  pl.BlockSpec(memory_space=pl.ANY)],
        out_specs=tail_spec,
        scratch_shapes=[pltpu.VMEM((N_FOX_HEADS * rem, 1), F32),
                        pltpu.VMEM((N_FOX_HEADS * rem, 1), F32),
                        pltpu.VMEM((N_FOX_HEADS * rem, d), F32)],
        input_output_aliases={4: 0},
        compiler_params=pltpu.CompilerParams(dimension_semantics=("arbitrary",),
                                             vmem_limit_bytes=VMEM_LIMIT),
        name="fox_attention_tail",
    )(q, k, v, c_rows, o)


def _out_kernel(x_ref, o_ref, gate_ref, w_ref, g_ref, b_ref, y_ref):
    mix = _dot((o_ref[0] * gate_ref[0]).astype(BF16), w_ref[...])
    y_ref[0] = _layer_norm_rows(DN_ALPHA * x_ref[0] + mix, g_ref[...], b_ref[...])


def _out_projection(h, o, gate, w, ln_g, ln_b):
    bsz, length, d = h.shape
    tl = ROW_TILE
    return pl.pallas_call(
        _out_kernel,
        out_shape=jax.ShapeDtypeStruct(h.shape, F32),
        grid=(bsz, length // tl),
        in_specs=[_row_spec(tl, d), _row_spec(tl, d), _row_spec(tl, d), _const_spec(w.shape),
                  _const_spec(ln_g.shape), _const_spec(ln_b.shape)],
        out_specs=_row_spec(tl, d),
        compiler_params=_params(),
        name="attn_out_projection",
    )(h, o, gate, w, ln_g, ln_b)


def _gate_weights(w_r, w_i):
    per = RNN_GROUP // LRU_BLOCK
    eye = jnp.eye(per, dtype=w_r.dtype)

    def block_diag(w):
        w = w.reshape(N_RNN_GROUPS, per, LRU_BLOCK, LRU_BLOCK)
        return jnp.einsum('gacd,ab->gacbd', w, eye).reshape(N_RNN_GROUPS, RNN_GROUP, RNN_GROUP)

    return jnp.concatenate([block_diag(w_r), block_diag(w_i)], axis=-1).astype(BF16)


def _row(v):
    return v.reshape(1, -1).astype(F32)


def kernel(x, meta, a_w_in, a_conv_w, a_conv_b, a_w_r, a_b_r, a_w_i, a_b_i, a_lambda, a_w_out, kv_w, kv_f_b, b_w_in, b_w_out, f_w_in, f_conv_w, f_conv_b, f_w_out, ln1_g, ln1_b, ln2_g, ln2_b):
    bsz, seq, d = x.shape
    length = seq + N_META
    assert d == D_MODEL and length % ROW_TILE == 0
    h = jnp.concatenate([jnp.broadcast_to(meta.astype(x.dtype), (bsz, N_META, d)), x], axis=1)

    k = v = c_rows = None
    for layer in range(DEPTH):
        if layer < N_A_LAYERS:
            h = _recurrent_layer(
                h, a_w_in[layer].astype(BF16), a_conv_w[layer], _row(a_conv_b[layer]),
                _gate_weights(a_w_r[layer], a_w_i[layer]), _row(a_b_r[layer]), _row(a_b_i[layer]),
                _row(a_lambda[layer]), a_w_out[layer].astype(BF16),
                _row(ln1_g[layer]), _row(ln1_b[layer]))
        else:
            if layer == N_A_LAYERS:
                wf = jnp.pad(kv_w[:, 2 * d:], ((0, 0), (0, LANES - N_FOX_HEADS))).astype(BF16)
                fb = jnp.pad(kv_f_b, (0, LANES - N_FOX_HEADS)).reshape(1, LANES).astype(F32)
                k, v, c_cols = _kv_projection(h, kv_w[:, :d].astype(BF16),
                                              kv_w[:, d:2 * d].astype(BF16), wf, fb)
                n_blk = -(-length // ATT_TILE)
                c_rows = jnp.transpose(c_cols[:, :, :N_FOX_HEADS], (0, 2, 1))
                c_rows = jnp.pad(c_rows, ((0, 0), (0, 0), (0, n_blk * ATT_TILE - length)))
                c_rows = c_rows.reshape(bsz, N_FOX_HEADS, n_blk, 1, ATT_TILE)
                c_pairs = jnp.transpose(
                    c_cols[:, :, :N_FOX_HEADS].reshape(bsz, length, -1, HEADS_PER_STEP), (0, 2, 1, 3))
            j = layer - N_A_LAYERS
            q, gate = _q_projection(h, b_w_in[j].astype(BF16))
            o = _attention(q, k, v, c_rows, c_pairs)
            h = _out_projection(h, o, gate, b_w_out[j].astype(BF16),
                                _row(ln1_g[layer]), _row(ln1_b[layer]))
        h = _ffn_layer(h, f_w_in[layer].astype(BF16), f_conv_w[layer], _row(f_conv_b[layer]),
                       f_w_out[layer].astype(BF16), _row(ln2_g[layer]), _row(ln2_b[layer]))
    return h[:, N_META:]
```

```python
import functools
import math

import jax
import jax.numpy as jnp
from jax import lax
from jax.experimental import pallas as pl
from jax.experimental.pallas import tpu as pltpu

D_MODEL = 1024
DEPTH = 4
N_META = 16
N_A_LAYERS = DEPTH // 2
D_RNN = 3 * D_MODEL // 2
N_LRU_BLOCKS = 16
LRU_BLOCK = D_RNN // N_LRU_BLOCKS
LRU_C = 8.0
CONV_A_WIDTH = 4
N_FOX_HEADS = 16
FOX_HEAD_DIM = D_MODEL // N_FOX_HEADS
D_FF = 2816
CONV_F_WIDTH = 3
DN_ALPHA = (2 * DEPTH) ** 0.25
LN_EPS = 1e-5

SUBLANES = 8
LANES = 128
ROW_TILE = 688
RNN_GROUP = 4 * LRU_BLOCK
N_RNN_GROUPS = D_RNN // RNN_GROUP
FF_CHUNK = 256
N_FF_CHUNKS = D_FF // FF_CHUNK
FF_SCRATCH_SETS = 2
FF_ROW_BLOCKS = 3
ATT_TILE = 256
HEADS_PER_STEP = LANES // FOX_HEAD_DIM
MASK_VALUE = -1e30
VMEM_LIMIT = 56 * 1024 * 1024

F32 = jnp.float32
BF16 = jnp.bfloat16


def _dot(a, b):
    return jnp.dot(a, b, preferred_element_type=F32)


def _layer_norm_rows(z, g, b):
    mu = jnp.mean(z, axis=-1, keepdims=True)
    zc = z - mu
    var = jnp.mean(zc * zc, axis=-1, keepdims=True)
    return zc * lax.rsqrt(var + LN_EPS) * g + b


def _gelu_tanh(x):
    c = math.sqrt(2.0 / math.pi)
    return 0.5 * x * (1.0 + jnp.tanh(c * (x + 0.044715 * (x * x * x))))


def _sigmoid(x):
    return 1.0 / (1.0 + jnp.exp(-x))


def _softplus(x):
    return jnp.maximum(x, 0.0) + jnp.log1p(jnp.exp(-jnp.abs(x)))


def _row_blocks(n_rows, n_blocks, align=16):
    per = -(-n_rows // (n_blocks * align)) * align
    return [(r, min(r + per, n_rows)) for r in range(0, n_rows, per)]


def _const_spec(shape):
    return pl.BlockSpec(shape, lambda *_: (0,) * len(shape), pipeline_mode=pl.Buffered(1))


def _row_spec(tl, width):
    return pl.BlockSpec((1, tl, width), lambda b, t: (b, t, 0))


def _params():
    return pltpu.CompilerParams(dimension_semantics=("arbitrary", "arbitrary"),
                                vmem_limit_bytes=VMEM_LIMIT)


def _segment_rows(buf, slab, start, seg):
    return buf[slab, pl.ds(start, SUBLANES, stride=seg), :]


def _causal_conv_interleaved(buf, slab, width, seg, w_ref, b_ref, cols):
    first = SUBLANES - (width - 1)
    cat = jnp.concatenate([_segment_rows(buf, slab, first + q, seg)
                           for q in range(seg + width - 1)], axis=0)
    out = b_ref[:, cols]
    for k in range(width):
        out = out + cat[k * SUBLANES:(k + seg) * SUBLANES, :] * w_ref[k:k + 1, cols]
    return out


def _store_time_order(buf, slab, x, seg):
    for p in range(seg):
        buf[slab, pl.ds(p, SUBLANES, stride=seg), :] = x[p * SUBLANES:(p + 1) * SUBLANES, :]


def _scan_segments(a, u, carry, seg):
    width = a.shape[1]
    h = jnp.zeros((SUBLANES, width), F32)
    prod = jnp.ones((SUBLANES, width), F32)
    local, prods = [], []
    for p in range(seg):
        blk = slice(p * SUBLANES, (p + 1) * SUBLANES)
        h = a[blk] * h + u[blk]
        prod = a[blk] * prod
        local.append(h)
        prods.append(prod)
    row = lax.broadcasted_iota(jnp.int32, (SUBLANES, width), 0)
    h_in = jnp.broadcast_to(carry, (SUBLANES, width))
    for _ in range(SUBLANES - 1):
        h_in = jnp.where(row == 0, carry, pltpu.roll(h + prod * h_in, 1, 0))
    states = [local[p] + prods[p] * h_in for p in range(seg)]
    return states, states[-1][SUBLANES - 1:SUBLANES, :]


def _recurrent_kernel(x_ref, w_in_ref, cw_ref, cb_ref, wg_ref, br_ref, bi_ref, lam_ref,
                      w_out_ref, g_ref, b_ref, o_ref,
                      tail_ref, hc_ref, p_buf, h_buf, y_buf):
    tl = x_ref.shape[1]
    seg = tl // SUBLANES
    slabs = RNN_GROUP // LANES

    @pl.when(pl.program_id(1) == 0)
    def _():
        tail_ref[...] = jnp.zeros_like(tail_ref)
        hc_ref[...] = jnp.zeros_like(hc_ref)

    x = x_ref[0]
    xb = x.astype(BF16)
    for g in range(N_RNN_GROUPS):
        c0, c1 = g * RNN_GROUP, (g + 1) * RNN_GROUP
        gate = _dot(xb, w_in_ref[:, c0:c1])
        rec = _dot(xb, w_in_ref[:, D_RNN + c0:D_RNN + c1])
        tail = tail_ref[:, c0:c1]
        tail_ref[:, c0:c1] = rec[tl - SUBLANES:, :]
        cols = []
        for c in range(slabs):
            lanes = slice(c * LANES, (c + 1) * LANES)
            p_buf[g * slabs + c, 0:SUBLANES, :] = tail[:, lanes]
            p_buf[g * slabs + c, SUBLANES:, :] = rec[:, lanes]
            cols.append(_causal_conv_interleaved(
                p_buf, g * slabs + c, CONV_A_WIDTH, seg, cw_ref, cb_ref,
                slice(c0 + c * LANES, c0 + (c + 1) * LANES)))
        rc = jnp.concatenate(cols, axis=1)
        ri = _dot(rc.astype(BF16), wg_ref[g])
        r = _sigmoid(ri[:, :RNN_GROUP] + br_ref[:, c0:c1])
        i = _sigmoid(ri[:, RNN_GROUP:] + bi_ref[:, c0:c1])
        log_a = (-LRU_C) * r * _softplus(-lam_ref[:, c0:c1])
        a = jnp.exp(log_a)
        t = -jnp.tanh(log_a)
        u = jnp.sqrt(2.0 * t / (1.0 + t)) * (i * rc)
        states, hc_ref[:, c0:c1] = _scan_segments(a, u, hc_ref[:, c0:c1], seg)
        h_all = jnp.concatenate(states, axis=0)
        for c in range(slabs):
            _store_time_order(h_buf, g * slabs + c, h_all[:, c * LANES:(c + 1) * LANES], seg)
        h_time = jnp.concatenate([h_buf[g * slabs + c] for c in range(slabs)], axis=1)
        y_buf[:, c0:c1] = (_gelu_tanh(gate) * h_time).astype(BF16)
    mix = _dot(y_buf[...], w_out_ref[...])
    o_ref[0] = _layer_norm_rows(DN_ALPHA * x + mix, g_ref[...], b_ref[...])


def _recurrent_layer(h, w_in, cw, cb, wg, br, bi, lam, w_out, ln_g, ln_b):
    bsz, length, d = h.shape
    tl = ROW_TILE
    return pl.pallas_call(
        _recurrent_kernel,
        out_shape=jax.ShapeDtypeStruct(h.shape, F32),
        grid=(bsz, length // tl),
        in_specs=[_row_spec(tl, d), _const_spec(w_in.shape), _const_spec(cw.shape),
                  _const_spec(cb.shape), _const_spec(wg.shape), _const_spec(br.shape),
                  _const_spec(bi.shape), _const_spec(lam.shape), _const_spec(w_out.shape),
                  _const_spec(ln_g.shape), _const_spec(ln_b.shape)],
        out_specs=_row_spec(tl, d),
        scratch_shapes=[pltpu.VMEM((SUBLANES, D_RNN), F32),
                        pltpu.VMEM((1, D_RNN), F32),
                        pltpu.VMEM((D_RNN // LANES, tl + SUBLANES, LANES), F32),
                        pltpu.VMEM((D_RNN // LANES, tl, LANES), F32),
                        pltpu.VMEM((tl, D_RNN), BF16)],
        compiler_params=_params(),
        name="recurrent_mixer",
    )(h, w_in, cw, cb, wg, br, bi, lam, w_out, ln_g, ln_b)


def _ffn_kernel(x_ref, w_in_ref, cw_ref, cb_ref, w_out_ref, g_ref, b_ref, o_ref,
                tail_ref, p_buf, t_buf, y_buf):
    tl = x_ref.shape[1]
    seg = tl // SUBLANES
    slabs = FF_CHUNK // LANES
    n_sets = p_buf.shape[0] // (2 * slabs)

    @pl.when(pl.program_id(1) == 0)
    def _():
        tail_ref[...] = jnp.zeros_like(tail_ref)

    x = x_ref[0]
    xb = x.astype(BF16)

    def conv_branch(c0, slab0):
        hid = jnp.concatenate([_dot(xb[r0:r1], w_in_ref[:, c0:c0 + FF_CHUNK])
                               for r0, r1 in _row_blocks(tl, FF_ROW_BLOCKS)], axis=0)
        tail = tail_ref[:, c0:c0 + FF_CHUNK]
        tail_ref[:, c0:c0 + FF_CHUNK] = hid[tl - SUBLANES:, :]
        out = []
        for s in range(slabs):
            lanes = slice(s * LANES, (s + 1) * LANES)
            p_buf[slab0 + s, 0:SUBLANES, :] = tail[:, lanes]
            p_buf[slab0 + s, SUBLANES:, :] = hid[:, lanes]
            cols = slice(c0 + s * LANES, c0 + (s + 1) * LANES)
            out.append(_causal_conv_interleaved(p_buf, slab0 + s, CONV_F_WIDTH, seg,
                                                cw_ref, cb_ref, cols))
        return out

    for c in range(N_FF_CHUNKS):
        c0 = c * FF_CHUNK
        base = (c % n_sets) * 2 * slabs
        gate = conv_branch(c0, base)
        val = conv_branch(D_FF + c0, base + slabs)
        for s in range(slabs):
            slab = (c % n_sets) * slabs + s
            _store_time_order(t_buf, slab, _gelu_tanh(gate[s]) * val[s], seg)
            y_buf[:, c0 + s * LANES:c0 + (s + 1) * LANES] = t_buf[slab].astype(BF16)
    ffn = _dot(y_buf[...], w_out_ref[...])
    o_ref[0] = _layer_norm_rows(DN_ALPHA * x + ffn, g_ref[...], b_ref[...])


def _ffn_layer(h, w_in, cw, cb, w_out, ln_g, ln_b):
    bsz, length, d = h.shape
    tl = ROW_TILE
    return pl.pallas_call(
        _ffn_kernel,
        out_shape=jax.ShapeDtypeStruct(h.shape, F32),
        grid=(bsz, length // tl),
        in_specs=[_row_spec(tl, d), _const_spec(w_in.shape), _const_spec(cw.shape),
                  _const_spec(cb.shape), _const_spec(w_out.shape),
                  _const_spec(ln_g.shape), _const_spec(ln_b.shape)],
        out_specs=_row_spec(tl, d),
        scratch_shapes=[pltpu.VMEM((SUBLANES, 2 * D_FF), F32),
                        pltpu.VMEM((FF_SCRATCH_SETS * 2 * FF_CHUNK // LANES, tl + SUBLANES, LANES), F32),
                        pltpu.VMEM((FF_SCRATCH_SETS * FF_CHUNK // LANES, tl, LANES), F32),
                        pltpu.VMEM((tl, D_FF), BF16)],
        compiler_params=_params(),
        name="conv_ffn",
    )(h, w_in, cw, cb, w_out, ln_g, ln_b)


def _kv_kernel(x_ref, wk_ref, wv_ref, wf_ref, fb_ref, k_ref, v_ref, c_ref, carry_ref):
    tl = x_ref.shape[1]

    @pl.when(pl.program_id(1) == 0)
    def _():
        carry_ref[...] = jnp.zeros_like(carry_ref)

    xb = x_ref[0].astype(BF16)
    k_ref[0] = _dot(xb, wk_ref[...]).astype(BF16)
    v_ref[0] = _dot(xb, wv_ref[...]).astype(BF16)
    zf = _dot(xb, wf_ref[...]) + fb_ref[...]
    c = jnp.minimum(zf, 0.0) - jnp.log1p(jnp.exp(-jnp.abs(zf)))
    row = lax.broadcasted_iota(jnp.int32, c.shape, 0)
    d = 1
    while d < tl:
        c = c + jnp.where(row >= d, pltpu.roll(c, d, 0), 0.0)
        d *= 2
    c = c + carry_ref[...]
    carry_ref[...] = c[tl - 1:tl, :]
    c_ref[0] = c


def _kv_projection(h, wk, wv, wf, fb):
    bsz, length, d = h.shape
    tl = ROW_TILE
    return pl.pallas_call(
        _kv_kernel,
        out_shape=(jax.ShapeDtypeStruct(h.shape, BF16), jax.ShapeDtypeStruct(h.shape, BF16),
                   jax.ShapeDtypeStruct((bsz, length, LANES), F32)),
        grid=(bsz, length // tl),
        in_specs=[_row_spec(tl, d), _const_spec(wk.shape), _const_spec(wv.shape),
                  _const_spec(wf.shape), _const_spec(fb.shape)],
        out_specs=(_row_spec(tl, d), _row_spec(tl, d), _row_spec(tl, LANES)),
        scratch_shapes=[pltpu.VMEM((1, LANES), F32)],
        compiler_params=_params(),
        name="kv_projection",
    )(h, wk, wv, wf, fb)


def _q_kernel(x_ref, w_ref, q_ref, gate_ref):
    xb = x_ref[0].astype(BF16)
    scale = FOX_HEAD_DIM ** -0.5
    q_ref[0] = (_dot(xb, w_ref[:, :D_MODEL]) * scale).astype(BF16)
    gate_ref[0] = _sigmoid(_dot(xb, w_ref[:, D_MODEL:]))


def _q_projection(h, w):
    bsz, length, d = h.shape
    tl = ROW_TILE
    return pl.pallas_call(
        _q_kernel,
        out_shape=(jax.ShapeDtypeStruct(h.shape, BF16), jax.ShapeDtypeStruct(h.shape, F32)),
        grid=(bsz, length // tl),
        in_specs=[_row_spec(tl, d), _const_spec(w.shape)],
        out_specs=(_row_spec(tl, d), _row_spec(tl, d)),
        compiler_params=_params(),
        name="q_projection",
    )(h, w)


def _nt_dot(a, b):
    return lax.dot_general(a, b, (((1,), (1,)), ((), ())), preferred_element_type=F32)


def _attention_kernel(q_ref, k_ref, v_ref, c_ref, o_ref,
                      s_buf0, s_buf1, p_buf0, p_buf1, a_buf0, a_buf1, m_buf, acc_buf, bias_buf,
                      crep_buf, vt_buf):
    t = ATT_TILE
    n_tiles = (q_ref.shape[1] // t)
    n_pairs = n_tiles * (n_tiles + 1) // 2
    assert n_pairs % 2 == 0
    ones_rows = vt_buf.shape[1] - LANES
    lane = lax.broadcasted_iota(jnp.int32, (1, LANES), 1)
    head0 = lane < FOX_HEAD_DIM

    key =lax.broadcasted_iota(jnp.int32, (t, t), 0)
    qry = lax.broadcasted_iota(jnp.int32, (t, t), 1)
    bias_buf[0] = jnp.zeros((t, t), F32)
    bias_buf[1] = jnp.where(key <= qry, 0.0, MASK_VALUE)
    for jb in range(n_tiles):
        rows = slice(jb * t, (jb + 1) * t)
        vt_buf[jb, 0:LANES, :] = v_ref[0, rows, :].astype(F32).T.astype(BF16)
        vt_buf[jb, LANES:, :] = jnp.ones((ones_rows, t), BF16)
        for hh in range(HEADS_PER_STEP):
            crep_buf[hh, rows, :] = jnp.broadcast_to(c_ref[0, 0, rows, hh:hh + 1], (t, LANES))
    o_ref[0, n_tiles * t:, :] = jnp.zeros((o_ref.shape[1] - n_tiles * t, LANES), F32)
    for buf in (p_buf0, p_buf1, a_buf0, a_buf1, acc_buf):
        buf[...] = jnp.zeros_like(buf)
    m_buf[...] = jnp.full_like(m_buf, MASK_VALUE)

    def scores(qi, j, s_out):
        q = q_ref[0, pl.ds(pl.multiple_of(qi * t, t), t), :]
        zero = jnp.zeros_like(q)
        q2 = jnp.concatenate([jnp.where(head0, q, zero), jnp.where(head0, zero, q)], axis=0)
        s_out[...] = _nt_dot(k_ref[0, pl.ds(pl.multiple_of(j * t, t), t), :], q2)

    def softmax(qi, j, s_in, p_out, a_out):
        masked = (j == qi).astype(jnp.int32)
        keys = pl.ds(pl.multiple_of(j * t, t), t)
        first = j == 0
        for hh in range(HEADS_PER_STEP):
            for half in range(t // LANES):
                cols = slice(hh * t + half * LANES, hh * t + (half + 1) * LANES)
                s = (s_in[:, cols] - crep_buf[hh, keys, :]
                     + bias_buf[masked, :, half * LANES:(half + 1) * LANES])
                m_old = jnp.where(first, MASK_VALUE, m_buf[:, cols])
                m_new = jnp.maximum(m_old, jnp.max(s, axis=0, keepdims=True))
                m_buf[:, cols] = m_new
                a_out[:, cols] = jnp.exp(m_old - m_new)
                p_out[:, cols] = jnp.exp(s - m_new).astype(BF16)

    def accumulate(qi, j, p_in, a_in):
        acc_buf[qi] = acc_buf[qi] * a_in[...] + _dot(vt_buf[j], p_in[...])

    def next_pair(pair):
        qi, j = pair
        last = j == qi
        return jnp.minimum(jnp.where(last, qi + 1, qi), n_tiles - 1), jnp.where(last, 0, j + 1)

    def body(_, carry):
        done0, done1, cur0, cur1, nxt0 = carry
        nxt1 = next_pair(nxt0)
        accumulate(*done0, p_buf0, a_buf0)
        accumulate(*done1, p_buf1, a_buf1)
        softmax(*cur0, s_buf0, p_buf0, a_buf0)
        softmax(*cur1, s_buf1, p_buf1, a_buf1)
        scores(*nxt0, s_buf0)
        scores(*nxt1, s_buf1)
        return cur0, cur1, nxt0, nxt1, next_pair(nxt1)

    zero = jnp.int32(0)
    empty = (zero, zero + 1)
    pairs = [(zero, zero)]
    for _ in range(2):
        pairs.append(next_pair(pairs[-1]))
    scores(*pairs[0], s_buf0)
    scores(*pairs[1], s_buf1)
    carry = lax.fori_loop(0, n_pairs // 2, body, (empty, empty, *pairs))
    accumulate(*carry[0], p_buf0, a_buf0)
    accumulate(*carry[1], p_buf1, a_buf1)

    head0_rows = lax.broadcasted_iota(jnp.int32, (LANES, 1), 0) < FOX_HEAD_DIM
    for qi in range(n_tiles):
        o2 = acc_buf[qi, 0:LANES, :] / acc_buf[qi, LANES:LANES + 1, :]
        o_ref[0, qi * t:(qi + 1) * t, :] = jnp.where(head0_rows, o2[:, 0:t], o2[:, t:]).T


def _attention_tail_kernel(q_ref, k_ref, v_ref, c_ref, o_in_ref, o_ref, m_buf, l_buf, acc_buf):
    del o_in_ref
    t = ATT_TILE
    rem = q_ref.shape[1]
    length = k_ref.shape[1]
    n_full = (length - rem) // t
    rows = N_FOX_HEADS * rem
    lane = lax.broadcasted_iota(jnp.int32, (1, D_MODEL), 1)
    head_lanes = [(lane >= h * FOX_HEAD_DIM) & (lane < (h + 1) * FOX_HEAD_DIM)
                  for h in range(N_FOX_HEADS)]
    q = q_ref[0]
    qs = jnp.concatenate([jnp.where(hl, q, jnp.zeros_like(q)) for hl in head_lanes], axis=0)
    m_buf[...] = jnp.full_like(m_buf, MASK_VALUE)
    l_buf[...] = jnp.zeros_like(l_buf)
    acc_buf[...] = jnp.zeros_like(acc_buf)

    def update(k_blk, v_blk, c_stack, mask):
        s = _nt_dot(qs, k_blk) - c_stack
        if mask is not None:
            s = jnp.where(mask, s, MASK_VALUE)
        m_old = m_buf[...]
        m_new = jnp.maximum(m_old, jnp.max(s, axis=1, keepdims=True))
        alpha = jnp.exp(m_old - m_new)
        p = jnp.exp(s - m_new)
        m_buf[...] = m_new
        l_buf[...] = alpha * l_buf[...] + jnp.sum(p, axis=1, keepdims=True)
        acc_buf[...] = alpha * acc_buf[...] + _dot(p.astype(BF16), v_blk)

    def c_stacked(j, width):
        return jnp.concatenate([jnp.broadcast_to(c_ref[0, h, j][:, :width], (rem, width))
                                for h in range(N_FOX_HEADS)], axis=0)

    def body(j, carry):
        start = pl.multiple_of(j * t, t)
        update(k_ref[0, pl.ds(start, t), :], v_ref[0, pl.ds(start, t), :], c_stacked(j, t), None)
        return carry

    lax.fori_loop(0, n_full, body, 0)
    r = lax.broadcasted_iota(jnp.int32, (rows, rem), 0) % rem
    cc = lax.broadcasted_iota(jnp.int32, (rows, rem), 1)
    update(k_ref[0, n_full * t:, :], v_ref[0, n_full * t:, :], c_stacked(n_full, rem), cc <= r)
    o_all = acc_buf[...] / l_buf[...]
    o = o_all[0:rem]
    for h in range(1, N_FOX_HEADS):
        o = jnp.where(head_lanes[h], o_all[h * rem:(h + 1) * rem], o)
    o_ref[0] = o


def _attention(q, k, v, c_rows, c_pairs):
    bsz, length, d = q.shape
    t = ATT_TILE
    n_blk = c_rows.shape[2]
    main = (length // t) * t
    rem = length - main
    qcols = HEADS_PER_STEP * t
    col_spec = pl.BlockSpec((1, length, LANES), lambda b, hp: (b, 0, hp))
    o = pl.pallas_call(
        _attention_kernel,
        out_shape=jax.ShapeDtypeStruct(q.shape, F32),
        grid=(bsz, N_FOX_HEADS // HEADS_PER_STEP),
        in_specs=[col_spec, col_spec, col_spec,
                  pl.BlockSpec((1, 1, length, HEADS_PER_STEP), lambda b, hp: (b, hp, 0, 0))],
        out_specs=col_spec,
        scratch_shapes=[pltpu.VMEM((t, qcols), F32), pltpu.VMEM((t, qcols), F32),
                        pltpu.VMEM((t, qcols), BF16), pltpu.VMEM((t, qcols), BF16),
                        pltpu.VMEM((1, qcols), F32), pltpu.VMEM((1, qcols), F32),
                        pltpu.VMEM((1, qcols), F32),
                        pltpu.VMEM((main // t, LANES + 2 * SUBLANES, qcols), F32),
                        pltpu.VMEM((2, t, t), F32),
                        pltpu.VMEM((HEADS_PER_STEP, main, LANES), F32),
                        pltpu.VMEM((main // t, LANES + 2 * SUBLANES, t), BF16)],
        compiler_params=_params(),
        name="fox_attention",
    )(q, k, v, c_pairs)
    if not rem:
        return o
    assert main % rem == 0
    full_spec = pl.BlockSpec((1, length, d), lambda b: (b, 0, 0))
    tail_spec = pl.BlockSpec((1, rem, d), lambda b: (b, main // rem, 0))
    return pl.pallas_call(
        _attention_tail_kernel,
        out_shape=jax.ShapeDtypeStruct(q.shape, F32),
        grid=(bsz,),
        in_specs=[tail_spec, full_spec, full_spec,
                  pl.BlockSpec((1, N_FOX_HEADS, n_blk, 1, t), lambda b: (b, 0, 0, 0, 0)),
                  pl.BlockSpec(memory_space=pl.ANY)],
        out_specs=tail_spec,
        scratch_shapes=[pltpu.VMEM((N_FOX_HEADS * rem, 1), F32),
                        pltpu.VMEM((N_FOX_HEADS * rem, 1), F32),
                        pltpu.VMEM((N_FOX_HEADS * rem, d), F32)],
        input_output_aliases={4: 0},
        compiler_params=pltpu.CompilerParams(dimension_semantics=("arbitrary",),
                                             vmem_limit_bytes=VMEM_LIMIT),
        name="fox_attention_tail",
    )(q, k, v, c_rows, o)


def _out_kernel(x_ref, o_ref, gate_ref, w_ref, g_ref, b_ref, y_ref):
    mix = _dot((o_ref[0] * gate_ref[0]).astype(BF16), w_ref[...])
    y_ref[0] = _layer_norm_rows(DN_ALPHA * x_ref[0] + mix, g_ref[...], b_ref[...])


def _out_projection(h, o, gate, w, ln_g, ln_b):
    bsz, length, d = h.shape
    tl = ROW_TILE
    return pl.pallas_call(
        _out_kernel,
        out_shape=jax.ShapeDtypeStruct(h.shape, F32),
        grid=(bsz, length // tl),
        in_specs=[_row_spec(tl, d), _row_spec(tl, d), _row_spec(tl, d), _const_spec(w.shape),
                  _const_spec(ln_g.shape), _const_spec(ln_b.shape)],
        out_specs=_row_spec(tl, d),
        compiler_params=_params(),
        name="attn_out_projection",
    )(h, o, gate, w, ln_g, ln_b)


def _gate_weights(w_r, w_i):
    per = RNN_GROUP // LRU_BLOCK
    eye = jnp.eye(per, dtype=w_r.dtype)

    def block_diag(w):
        w = w.reshape(N_RNN_GROUPS, per, LRU_BLOCK, LRU_BLOCK)
        return jnp.einsum('gacd,ab->gacbd', w, eye).reshape(N_RNN_GROUPS, RNN_GROUP, RNN_GROUP)

    return jnp.concatenate([block_diag(w_r), block_diag(w_i)], axis=-1).astype(BF16)


def _row(v):
    return v.reshape(1, -1).astype(F32)


def kernel(x, meta, a_w_in, a_conv_w, a_conv_b, a_w_r, a_b_r, a_w_i, a_b_i, a_lambda, a_w_out, kv_w, kv_f_b, b_w_in, b_w_out, f_w_in, f_conv_w, f_conv_b, f_w_out, ln1_g, ln1_b, ln2_g, ln2_b):
    bsz, seq, d = x.shape
    length = seq + N_META
    assert d == D_MODEL and length % ROW_TILE == 0
    h = jnp.concatenate([jnp.broadcast_to(meta.astype(x.dtype), (bsz, N_META, d)), x], axis=1)

    k = v = c_rows = None
    for layer in range(DEPTH):
        if layer < N_A_LAYERS:
            h = _recurrent_layer(
                h, a_w_in[layer].astype(BF16), a_conv_w[layer], _row(a_conv_b[layer]),
                _gate_weights(a_w_r[layer], a_w_i[layer]), _row(a_b_r[layer]), _row(a_b_i[layer]),
                _row(a_lambda[layer]), a_w_out[layer].astype(BF16),
                _row(ln1_g[layer]), _row(ln1_b[layer]))
        else:
            if layer == N_A_LAYERS:
                wf = jnp.pad(kv_w[:, 2 * d:], ((0, 0), (0, LANES - N_FOX_HEADS))).astype(BF16)
                fb = jnp.pad(kv_f_b, (0, LANES - N_FOX_HEADS)).reshape(1, LANES).astype(F32)
                k, v, c_cols = _kv_projection(h, kv_w[:, :d].astype(BF16),
                                              kv_w[:, d:2 * d].astype(BF16), wf, fb)
                n_blk = -(-length // ATT_TILE)
                c_rows = jnp.transpose(c_cols[:, :, :N_FOX_HEADS], (0, 2, 1))
                c_rows = jnp.pad(c_rows, ((0, 0), (0, 0), (0, n_blk * ATT_TILE - length)))
                c_rows = c_rows.reshape(bsz, N_FOX_HEADS, n_blk, 1, ATT_TILE)
                c_pairs = jnp.transpose(
                    c_cols[:, :, :N_FOX_HEADS].reshape(bsz, length, -1, HEADS_PER_STEP), (0, 2, 1, 3))
            j = layer - N_A_LAYERS
            q, gate = _q_projection(h, b_w_in[j].astype(BF16))
            o = _attention(q, k, v, c_rows, c_pairs)
            h = _out_projection(h, o, gate, b_w_out[j].astype(BF16),
                                _row(ln1_g[layer]), _row(ln1_b[layer]))
        h = _ffn_layer(h, f_w_in[layer].astype(BF16), f_conv_w[layer], _row(f_conv_b[layer]),
                       f_w_out[layer].astype(BF16), _row(ln2_g[layer]), _row(ln2_b[layer]))
    return h[:, N_META:]
```

```python
import functools
import math

import jax
import jax.numpy as jnp
from jax import lax
from jax.experimental import pallas as pl
from jax.experimental.pallas import tpu as pltpu

D_MODEL = 1024
DEPTH = 4
N_META = 16
N_A_LAYERS = DEPTH // 2
D_RNN = 3 * D_MODEL // 2
N_LRU_BLOCKS = 16
LRU_BLOCK = D_RNN // N_LRU_BLOCKS
LRU_C = 8.0
CONV_A_WIDTH = 4
N_FOX_HEADS = 16
FOX_HEAD_DIM = D_MODEL // N_FOX_HEADS
D_FF = 2816
CONV_F_WIDTH = 3
DN_ALPHA = (2 * DEPTH) ** 0.25
LN_EPS = 1e-5

SUBLANES = 8
LANES = 128
ROW_TILE = 688
RNN_GROUP = 4 * LRU_BLOCK
N_RNN_GROUPS = D_RNN // RNN_GROUP
FF_CHUNK = 256
N_FF_CHUNKS = D_FF // FF_CHUNK
ATT_TILE = 256
ATT_UNROLL = 4
HEADS_PER_STEP = LANES // FOX_HEAD_DIM
MASK_VALUE = -1e30
VMEM_LIMIT = 56 * 1024 * 1024

F32 = jnp.float32
BF16 = jnp.bfloat16


def _dot(a, b):
    return jnp.dot(a, b, preferred_element_type=F32)


def _layer_norm_rows(z, g, b):
    mu = jnp.mean(z, axis=-1, keepdims=True)
    zc = z - mu
    var = jnp.mean(zc * zc, axis=-1, keepdims=True)
    return zc * lax.rsqrt(var + LN_EPS) * g + b


def _gelu_tanh(x):
    c = math.sqrt(2.0 / math.pi)
    return 0.5 * x * (1.0 + jnp.tanh(c * (x + 0.044715 * (x * x * x))))


def _sigmoid(x):
    return 1.0 / (1.0 + jnp.exp(-x))


def _softplus(x):
    return jnp.maximum(x, 0.0) + jnp.log1p(jnp.exp(-jnp.abs(x)))


def _const_spec(shape):
    return pl.BlockSpec(shape, lambda *_: (0,) * len(shape), pipeline_mode=pl.Buffered(1))


def _row_spec(tl, width):
    return pl.BlockSpec((1, tl, width), lambda b, t: (b, t, 0))


def _params():
    return pltpu.CompilerParams(dimension_semantics=("arbitrary", "arbitrary"),
                                vmem_limit_bytes=VMEM_LIMIT)


def _segment_rows(buf, slab, start, seg):
    return buf[slab, pl.ds(start, SUBLANES, stride=seg), :]


def _causal_conv_interleaved(buf, slab, width, seg, w_ref, b_ref, cols):
    first = SUBLANES - (width - 1)
    cat = jnp.concatenate([_segment_rows(buf, slab, first + q, seg)
                           for q in range(seg + width - 1)], axis=0)
    out = b_ref[:, cols]
    for k in range(width):
        out = out + cat[k * SUBLANES:(k + seg) * SUBLANES, :] * w_ref[k:k + 1, cols]
    return out


def _store_time_order(buf, slab, x, seg):
    for p in range(seg):
        buf[slab, pl.ds(p, SUBLANES, stride=seg), :] = x[p * SUBLANES:(p + 1) * SUBLANES, :]


def _scan_segments(a, u, carry, seg):
    width = a.shape[1]
    h = jnp.zeros((SUBLANES, width), F32)
    prod = jnp.ones((SUBLANES, width), F32)
    local, prods = [], []
    for p in range(seg):
        blk = slice(p * SUBLANES, (p + 1) * SUBLANES)
        h = a[blk] * h + u[blk]
        prod = a[blk] * prod
        local.append(h)
        prods.append(prod)
    row = lax.broadcasted_iota(jnp.int32, (SUBLANES, width), 0)
    h_in = jnp.broadcast_to(carry, (SUBLANES, width))
    for _ in range(SUBLANES - 1):
        h_in = jnp.where(row == 0, carry, pltpu.roll(h + prod * h_in, 1, 0))
    states = [local[p] + prods[p] * h_in for p in range(seg)]
    return states, states[-1][SUBLANES - 1:SUBLANES, :]


def _recurrent_kernel(x_ref, w_in_ref, cw_ref, cb_ref, wg_ref, br_ref, bi_ref, lam_ref,
                      w_out_ref, g_ref, b_ref, o_ref,
                      tail_ref, hc_ref, p_buf, h_buf, y_buf):
    tl = x_ref.shape[1]
    seg = tl // SUBLANES
    slabs = RNN_GROUP // LANES

    @pl.when(pl.program_id(1) == 0)
    def _():
        tail_ref[...] = jnp.zeros_like(tail_ref)
        hc_ref[...] = jnp.zeros_like(hc_ref)

    x = x_ref[0]
    xb = x.astype(BF16)
    for g in range(N_RNN_GROUPS):
        c0, c1 = g * RNN_GROUP, (g + 1) * RNN_GROUP
        gate = _dot(xb, w_in_ref[:, c0:c1])
        rec = _dot(xb, w_in_ref[:, D_RNN + c0:D_RNN + c1])
        tail = tail_ref[:, c0:c1]
        tail_ref[:, c0:c1] = rec[tl - SUBLANES:, :]
        cols = []
        for c in range(slabs):
            lanes = slice(c * LANES, (c + 1) * LANES)
            p_buf[g * slabs + c, 0:SUBLANES, :] = tail[:, lanes]
            p_buf[g * slabs + c, SUBLANES:, :] = rec[:, lanes]
            cols.append(_causal_conv_interleaved(
                p_buf, g * slabs + c, CONV_A_WIDTH, seg, cw_ref, cb_ref,
                slice(c0 + c * LANES, c0 + (c + 1) * LANES)))
        rc = jnp.concatenate(cols, axis=1)
        ri = _dot(rc.astype(BF16), wg_ref[g])
        r = _sigmoid(ri[:, :RNN_GROUP] + br_ref[:, c0:c1])
        i = _sigmoid(ri[:, RNN_GROUP:] + bi_ref[:, c0:c1])
        log_a = (-LRU_C) * r * _softplus(-lam_ref[:, c0:c1])
        a = jnp.exp(log_a)
        t = -jnp.tanh(log_a)
        u = jnp.sqrt(2.0 * t / (1.0 + t)) * (i * rc)
        states, hc_ref[:, c0:c1] = _scan_segments(a, u, hc_ref[:, c0:c1], seg)
        h_all = jnp.concatenate(states, axis=0)
        for c in range(slabs):
            _store_time_order(h_buf, g * slabs + c, h_all[:, c * LANES:(c + 1) * LANES], seg)
        h_time = jnp.concatenate([h_buf[g * slabs + c] for c in range(slabs)], axis=1)
        y_buf[:, c0:c1] = (_gelu_tanh(gate) * h_time).astype(BF16)
    mix = _dot(y_buf[...], w_out_ref[...])
    o_ref[0] = _layer_norm_rows(DN_ALPHA * x + mix, g_ref[...], b_ref[...])


def _recurrent_layer(h, w_in, cw, cb, wg, br, bi, lam, w_out, ln_g, ln_b):
    bsz, length, d = h.shape
    tl = ROW_TILE
    return pl.pallas_call(
        _recurrent_kernel,
        out_shape=jax.ShapeDtypeStruct(h.shape, F32),
        grid=(bsz, length // tl),
        in_specs=[_row_spec(tl, d), _const_spec(w_in.shape), _const_spec(cw.shape),
                  _const_spec(cb.shape), _const_spec(wg.shape), _const_spec(br.shape),
                  _const_spec(bi.shape), _const_spec(lam.shape), _const_spec(w_out.shape),
                  _const_spec(ln_g.shape), _const_spec(ln_b.shape)],
        out_specs=_row_spec(tl, d),
        scratch_shapes=[pltpu.VMEM((SUBLANES, D_RNN), F32),
                        pltpu.VMEM((1, D_RNN), F32),
                        pltpu.VMEM((D_RNN // LANES, tl + SUBLANES, LANES), F32),
                        pltpu.VMEM((D_RNN // LANES, tl, LANES), F32),
                        pltpu.VMEM((tl, D_RNN), BF16)],
        compiler_params=_params(),
        name="recurrent_mixer",
    )(h, w_in, cw, cb, wg, br, bi, lam, w_out, ln_g, ln_b)


def _ffn_kernel(x_ref, w_in_ref, cw_ref, cb_ref, w_out_ref, g_ref, b_ref, o_ref,
                tail_ref, p_buf, y_buf):
    tl = x_ref.shape[1]

    @pl.when(pl.program_id(1) == 0)
    def _():
        tail_ref[...] = jnp.zeros_like(tail_ref)

    x = x_ref[0]
    xb = x.astype(BF16)

    def conv_branch(c0, c1):
        hid = _dot(xb, w_in_ref[:, c0:c1])
        p_buf[0:SUBLANES, :] = tail_ref[:, c0:c1]
        p_buf[SUBLANES:, :] = hid
        tail_ref[:, c0:c1] = hid[tl - SUBLANES:, :]
        out = cb_ref[:, c0:c1]
        for k in range(CONV_F_WIDTH):
            off = SUBLANES - (CONV_F_WIDTH - 1) + k
            out = out + p_buf[off:off + tl, :] * cw_ref[k:k + 1, c0:c1]
        return out

    for c in range(N_FF_CHUNKS):
        c0, c1 = c * FF_CHUNK, (c + 1) * FF_CHUNK
        gate = conv_branch(c0, c1)
        val = conv_branch(D_FF + c0, D_FF + c1)
        y_buf[:, c0:c1] = (_gelu_tanh(gate) * val).astype(BF16)
    ffn = _dot(y_buf[...], w_out_ref[...])
    o_ref[0] = _layer_norm_rows(DN_ALPHA * x + ffn, g_ref[...], b_ref[...])


def _ffn_layer(h, w_in, cw, cb, w_out, ln_g, ln_b):
    bsz, length, d = h.shape
    tl = ROW_TILE
    return pl.pallas_call(
        _ffn_kernel,
        out_shape=jax.ShapeDtypeStruct(h.shape, F32),
        grid=(bsz, length // tl),
        in_specs=[_row_spec(tl, d), _const_spec(w_in.shape), _const_spec(cw.shape),
                  _const_spec(cb.shape), _const_spec(w_out.shape),
                  _const_spec(ln_g.shape), _const_spec(ln_b.shape)],
        out_specs=_row_spec(tl, d),
        scratch_shapes=[pltpu.VMEM((SUBLANES, 2 * D_FF), F32),
                        pltpu.VMEM((tl + SUBLANES, FF_CHUNK), F32),
                        pltpu.VMEM((tl, D_FF), BF16)],
        compiler_params=_params(),
        name="conv_ffn",
    )(h, w_in, cw, cb, w_out, ln_g, ln_b)


def _kv_kernel(x_ref, wk_ref, wv_ref, wf_ref, fb_ref, k_ref, v_ref, c_ref, carry_ref):
    tl = x_ref.shape[1]

    @pl.when(pl.program_id(1) == 0)
    def _():
        carry_ref[...] = jnp.zeros_like(carry_ref)

    xb = x_ref[0].astype(BF16)
    k_ref[0] = _dot(xb, wk_ref[...]).astype(BF16)
    v_ref[0] = _dot(xb, wv_ref[...]).astype(BF16)
    zf = _dot(xb, wf_ref[...]) + fb_ref[...]
    c = jnp.minimum(zf, 0.0) - jnp.log1p(jnp.exp(-jnp.abs(zf)))
    row = lax.broadcasted_iota(jnp.int32, c.shape, 0)
    d = 1
    while d < tl:
        c = c + jnp.where(row >= d, pltpu.roll(c, d, 0), 0.0)
        d *= 2
    c = c + carry_ref[...]
    carry_ref[...] = c[tl - 1:tl, :]
    c_ref[0] = c


def _kv_projection(h, wk, wv, wf, fb):
    bsz, length, d = h.shape
    tl = ROW_TILE
    return pl.pallas_call(
        _kv_kernel,
        out_shape=(jax.ShapeDtypeStruct(h.shape, BF16), jax.ShapeDtypeStruct(h.shape, BF16),
                   jax.ShapeDtypeStruct((bsz, length, LANES), F32)),
        grid=(bsz, length // tl),
        in_specs=[_row_spec(tl, d), _const_spec(wk.shape), _const_spec(wv.shape),
                  _const_spec(wf.shape), _const_spec(fb.shape)],
        out_specs=(_row_spec(tl, d), _row_spec(tl, d), _row_spec(tl, LANES)),
        scratch_shapes=[pltpu.VMEM((1, LANES), F32)],
        compiler_params=_params(),
        name="kv_projection",
    )(h, wk, wv, wf, fb)


def _q_kernel(x_ref, w_ref, q_ref, gate_ref):
    xb = x_ref[0].astype(BF16)
    scale = FOX_HEAD_DIM ** -0.5
    q_ref[0] = (_dot(xb, w_ref[:, :D_MODEL]) * scale).astype(BF16)
    gate_ref[0] = _sigmoid(_dot(xb, w_ref[:, D_MODEL:])).astype(BF16)


def _q_projection(h, w):
    bsz, length, d = h.shape
    tl = ROW_TILE
    return pl.pallas_call(
        _q_kernel,
        out_shape=(jax.ShapeDtypeStruct(h.shape, BF16), jax.ShapeDtypeStruct(h.shape, BF16)),
        grid=(bsz, length // tl),
        in_specs=[_row_spec(tl, d), _const_spec(w.shape)],
        out_specs=(_row_spec(tl, d), _row_spec(tl, d)),
        compiler_params=_params(),
        name="q_projection",
    )(h, w)


def _nt_dot(a, b):
    return lax.dot_general(a, b, (((1,), (1,)), ((), ())), preferred_element_type=F32)


def _attention_kernel(q_ref, k_ref, v_ref, c_ref, o_ref, *scratch):
    t = ATT_TILE
    u = ATT_UNROLL
    s_bufs, p_bufs, a_bufs = scratch[0:u], scratch[u:2 * u], scratch[2 * u:3 * u]
    m_buf, acc_buf, bias_buf, crep_buf, vt_buf = scratch[3 * u:]
    n_tiles = (q_ref.shape[1] // t)
    n_pairs = n_tiles * (n_tiles + 1) // 2
    assert n_pairs % u == 0
    ones_rows = vt_buf.shape[1] - LANES
    lane = lax.broadcasted_iota(jnp.int32, (1, LANES), 1)
    head0 = lane < FOX_HEAD_DIM

    key =lax.broadcasted_iota(jnp.int32, (t, t), 0)
    qry = lax.broadcasted_iota(jnp.int32, (t, t), 1)
    bias_buf[0] = jnp.zeros((t, t), F32)
    bias_buf[1] = jnp.where(key <= qry, 0.0, MASK_VALUE)
    for jb in range(n_tiles):
        rows = slice(jb * t, (jb + 1) * t)
        vt_buf[jb, 0:LANES, :] = v_ref[0, rows, :].astype(F32).T.astype(BF16)
        vt_buf[jb, LANES:, :] = jnp.ones((ones_rows, t), BF16)
        for hh in range(HEADS_PER_STEP):
            crep_buf[hh, rows, :] = jnp.broadcast_to(c_ref[0, hh, jb], (LANES, t)).T
    o_ref[0, n_tiles * t:, :] = jnp.zeros((o_ref.shape[1] - n_tiles * t, LANES), BF16)
    for buf in (*p_bufs, *a_bufs, acc_buf):
        buf[...] = jnp.zeros_like(buf)
    m_buf[...] = jnp.full_like(m_buf, MASK_VALUE)

    def scores(qi, j, s_out):
        q = q_ref[0, pl.ds(pl.multiple_of(qi * t, t), t), :]
        zero = jnp.zeros_like(q)
        q2 = jnp.concatenate([jnp.where(head0, q, zero), jnp.where(head0, zero, q)], axis=0)
        s_out[...] = _nt_dot(k_ref[0, pl.ds(pl.multiple_of(j * t, t), t), :], q2)

    def softmax(qi, j, s_in, p_out, a_out):
        masked = (j == qi).astype(jnp.int32)
        keys = pl.ds(pl.multiple_of(j * t, t), t)
        first = j == 0
        for hh in range(HEADS_PER_STEP):
            for half in range(t // LANES):
                cols = slice(hh * t + half * LANES, hh * t + (half + 1) * LANES)
                s = (s_in[:, cols] - crep_buf[hh, keys, :]
                     + bias_buf[masked, :, half * LANES:(half + 1) * LANES])
                m_old = jnp.where(first, MASK_VALUE, m_buf[:, cols])
                m_new = jnp.maximum(m_old, jnp.max(s, axis=0, keepdims=True))
                m_buf[:, cols] = m_new
                a_out[:, cols] = jnp.exp(m_old - m_new)
                p_out[:, cols] = jnp.exp(s - m_new).astype(BF16)

    def accumulate(qi, j, p_in, a_in):
        acc_buf[qi] = acc_buf[qi] * a_in[...] + _dot(vt_buf[j], p_in[...])

    def next_pair(pair):
        qi, j = pair
        last = j == qi
        return jnp.minimum(jnp.where(last, qi + 1, qi), n_tiles - 1), jnp.where(last, 0, j + 1)

    def following(pair, count):
        out = []
        for _ in range(count):
            pair = next_pair(pair)
            out.append(pair)
        return out

    def body(_, carry):
        done, cur, nxt0 = carry[0:u], carry[u:2 * u], carry[2 * u]
        nxt = [nxt0] + following(nxt0, u - 1)
        for i in range(u):
            accumulate(*done[i], p_bufs[i], a_bufs[i])
        for i in range(u):
            softmax(*cur[i], s_bufs[i], p_bufs[i], a_bufs[i])
        for i in range(u):
            scores(*nxt[i], s_bufs[i])
        return (*cur, *nxt, next_pair(nxt[-1]))

    zero = jnp.int32(0)
    empty = (zero, zero + 1)
    first = [(zero, zero)] + following((zero, zero), u)
    for i in range(u):
        scores(*first[i], s_bufs[i])
    carry = lax.fori_loop(0, n_pairs // u, body, (*([empty] * u), *first))
    for i in range(u):
        accumulate(*carry[i], p_bufs[i], a_bufs[i])

    head0_rows = lax.broadcasted_iota(jnp.int32, (LANES, 1), 0) < FOX_HEAD_DIM
    for qi in range(n_tiles):
        o2 = acc_buf[qi, 0:LANES, :] / acc_buf[qi, LANES:LANES + 1, :]
        o_ref[0, qi * t:(qi + 1) * t, :] = (
            jnp.where(head0_rows, o2[:, 0:t], o2[:, t:]).T.astype(BF16))


def _attention_tail_kernel(q_ref, k_ref, v_ref, c_ref, o_ref, m_buf, l_buf, acc_buf):
    t = ATT_TILE
    rem = q_ref.shape[1]
    length = k_ref.shape[1]
    n_full = (length - rem) // t
    rows = N_FOX_HEADS * rem
    lane = lax.broadcasted_iota(jnp.int32, (1, D_MODEL), 1)
    head_lanes = [(lane >= h * FOX_HEAD_DIM) & (lane < (h + 1) * FOX_HEAD_DIM)
                  for h in range(N_FOX_HEADS)]
    q = q_ref[0]
    qs = jnp.concatenate([jnp.where(hl, q, jnp.zeros_like(q)) for hl in head_lanes], axis=0)
    m_buf[...] = jnp.full_like(m_buf, MASK_VALUE)
    l_buf[...] = jnp.zeros_like(l_buf)
    acc_buf[...] = jnp.zeros_like(acc_buf)

    def update(k_blk, v_blk, c_stack, mask):
        s = _nt_dot(qs, k_blk) - c_stack
        if mask is not None:
            s = jnp.where(mask, s, MASK_VALUE)
        m_old = m_buf[...]
        m_new = jnp.maximum(m_old, jnp.max(s, axis=1, keepdims=True))
        alpha = jnp.exp(m_old - m_new)
        p = jnp.exp(s - m_new)
        m_buf[...] = m_new
        l_buf[...] = alpha * l_buf[...] + jnp.sum(p, axis=1, keepdims=True)
        acc_buf[...] = alpha * acc_buf[...] + _dot(p.astype(BF16), v_blk)

    def c_stacked(j, width):
        return jnp.concatenate([jnp.broadcast_to(c_ref[0, h, j][:, :width], (rem, width))
                                for h in range(N_FOX_HEADS)], axis=0)

    def body(j, carry):
        start = pl.multiple_of(j * t, t)
        update(k_ref[0, pl.ds(start, t), :], v_ref[0, pl.ds(start, t), :], c_stacked(j, t), None)
        return carry

    lax.fori_loop(0, n_full, body, 0)
    r = lax.broadcasted_iota(jnp.int32, (rows, rem), 0) % rem
    cc = lax.broadcasted_iota(jnp.int32, (rows, rem), 1)
    update(k_ref[0, n_full * t:, :], v_ref[0, n_full * t:, :], c_stacked(n_full, rem), cc <= r)
    o_all = acc_buf[...] / l_buf[...]
    o = o_all[0:rem]
    for h in range(1, N_FOX_HEADS):
        o = jnp.where(head_lanes[h], o_all[h * rem:(h + 1) * rem], o)
    o_ref[0] = o.astype(BF16)


def _attention(q, k, v, c_rows):
    bsz, length, d = q.shape
    t = ATT_TILE
    n_blk = c_rows.shape[2]
    main = (length // t) * t
    rem = length - main
    qcols = HEADS_PER_STEP * t
    col_spec = pl.BlockSpec((1, length, LANES), lambda b, hp: (b, 0, hp))
    o = pl.pallas_call(
        _attention_kernel,
        out_shape=jax.ShapeDtypeStruct(q.shape, BF16),
        grid=(bsz, N_FOX_HEADS // HEADS_PER_STEP),
        in_specs=[col_spec, col_spec, col_spec,
                  pl.BlockSpec((1, HEADS_PER_STEP, n_blk, 1, t), lambda b, hp: (b, hp, 0, 0, 0))],
        out_specs=col_spec,
        scratch_shapes=[*[pltpu.VMEM((t, qcols), F32)] * ATT_UNROLL,
                        *[pltpu.VMEM((t, qcols), BF16)] * ATT_UNROLL,
                        *[pltpu.VMEM((1, qcols), F32)] * ATT_UNROLL,
                        pltpu.VMEM((1, qcols), F32),
                        pltpu.VMEM((main // t, LANES + 2 * SUBLANES, qcols), F32),
                        pltpu.VMEM((2, t, t), F32),
                        pltpu.VMEM((HEADS_PER_STEP, main, LANES), F32),
                        pltpu.VMEM((main // t, LANES + 2 * SUBLANES, t), BF16)],
        compiler_params=_params(),
        name="fox_attention",
    )(q, k, v, c_rows)
    assert rem and main % rem == 0
    full_spec = pl.BlockSpec((1, length, d), lambda b: (b, 0, 0))
    o_tail = pl.pallas_call(
        _attention_tail_kernel,
        out_shape=jax.ShapeDtypeStruct((bsz, rem, d), BF16),
        grid=(bsz,),
        in_specs=[pl.BlockSpec((1, rem, d), lambda b: (b, main // rem, 0)), full_spec, full_spec,
                  pl.BlockSpec((1, N_FOX_HEADS, n_blk, 1, t), lambda b: (b, 0, 0, 0, 0))],
        out_specs=pl.BlockSpec((1, rem, d), lambda b: (b, 0, 0)),
        scratch_shapes=[pltpu.VMEM((N_FOX_HEADS * rem, 1), F32),
                        pltpu.VMEM((N_FOX_HEADS * rem, 1), F32),
                        pltpu.VMEM((N_FOX_HEADS * rem, d), F32)],
        compiler_params=pltpu.CompilerParams(dimension_semantics=("arbitrary",),
                                             vmem_limit_bytes=VMEM_LIMIT),
        name="fox_attention_tail",
    )(q, k, v, c_rows)
    return o, o_tail


def _out_kernel(x_ref, o_ref, o_tail_ref, gate_ref, w_ref, g_ref, b_ref, y_ref):
    tl = x_ref.shape[1]
    rem = o_tail_ref.shape[1]
    o = o_ref[0]
    is_last = pl.program_id(1) == pl.num_programs(1) - 1
    o = jnp.where(is_last, jnp.concatenate([o[:tl - rem], o_tail_ref[0]], axis=0), o)
    mix = _dot(o * gate_ref[0], w_ref[...])
    y_ref[0] = _layer_norm_rows(DN_ALPHA * x_ref[0] + mix, g_ref[...], b_ref[...])


def _out_projection(h, o, o_tail, gate, w, ln_g, ln_b):
    bsz, length, d = h.shape
    tl = ROW_TILE
    return pl.pallas_call(
        _out_kernel,
        out_shape=jax.ShapeDtypeStruct(h.shape, F32),
        grid=(bsz, length // tl),
        in_specs=[_row_spec(tl, d), _row_spec(tl, d),
                  pl.BlockSpec((1,) + o_tail.shape[1:], lambda b, t: (b, 0, 0)),
                  _row_spec(tl, d), _const_spec(w.shape),
                  _const_spec(ln_g.shape), _const_spec(ln_b.shape)],
        out_specs=_row_spec(tl, d),
        compiler_params=_params(),
        name="attn_out_projection",
    )(h, o, o_tail, gate, w, ln_g, ln_b)


def _gate_weights(w_r, w_i):
    per = RNN_GROUP // LRU_BLOCK
    eye = jnp.eye(per, dtype=w_r.dtype)

    def block_diag(w):
        w = w.reshape(N_RNN_GROUPS, per, LRU_BLOCK, LRU_BLOCK)
        return jnp.einsum('gacd,ab->gacbd', w, eye).reshape(N_RNN_GROUPS, RNN_GROUP, RNN_GROUP)

    return jnp.concatenate([block_diag(w_r), block_diag(w_i)], axis=-1).astype(BF16)


def _row(v):
    return v.reshape(1, -1).astype(F32)


def kernel(x, meta, a_w_in, a_conv_w, a_conv_b, a_w_r, a_b_r, a_w_i, a_b_i, a_lambda, a_w_out, kv_w, kv_f_b, b_w_in, b_w_out, f_w_in, f_conv_w, f_conv_b, f_w_out, ln1_g, ln1_b, ln2_g, ln2_b):
    bsz, seq, d = x.shape
    length = seq + N_META
    assert d == D_MODEL and length % ROW_TILE == 0
    h = jnp.concatenate([jnp.broadcast_to(meta.astype(x.dtype), (bsz, N_META, d)), x], axis=1)

    k = v = c_rows = None
    for layer in range(DEPTH):
        if layer < N_A_LAYERS:
            h = _recurrent_layer(
                h, a_w_in[layer].astype(BF16), a_conv_w[layer], _row(a_conv_b[layer]),
                _gate_weights(a_w_r[layer], a_w_i[layer]), _row(a_b_r[layer]), _row(a_b_i[layer]),
                _row(a_lambda[layer]), a_w_out[layer].astype(BF16),
                _row(ln1_g[layer]), _row(ln1_b[layer]))
        else:
            if layer == N_A_LAYERS:
                wf = jnp.pad(kv_w[:, 2 * d:], ((0, 0), (0, LANES - N_FOX_HEADS))).astype(BF16)
                fb = jnp.pad(kv_f_b, (0, LANES - N_FOX_HEADS)).reshape(1, LANES).astype(F32)
                k, v, c_cols = _kv_projection(h, kv_w[:, :d].astype(BF16),
                                              kv_w[:, d:2 * d].astype(BF16), wf, fb)
                n_blk = -(-length // ATT_TILE)
                c_rows = jnp.transpose(c_cols[:, :, :N_FOX_HEADS], (0, 2, 1))
                c_rows = jnp.pad(c_rows, ((0, 0), (0, 0), (0, n_blk * ATT_TILE - length)))
                c_rows = c_rows.reshape(bsz, N_FOX_HEADS, n_blk, 1, ATT_TILE)
            j = layer - N_A_LAYERS
            q, gate = _q_projection(h, b_w_in[j].astype(BF16))
            o, o_tail = _attention(q, k, v, c_rows)
            h = _out_projection(h, o, o_tail, gate, b_w_out[j].astype(BF16),
                                _row(ln1_g[layer]), _row(ln1_b[layer]))
        h = _ffn_layer(h, f_w_in[layer].astype(BF16), f_conv_w[layer], _row(f_conv_b[layer]),
                       f_w_out[layer].astype(BF16), _row(ln2_g[layer]), _row(ln2_b[layer]))
    return h[:, N_META:]
```

```python
import functools
import math

import jax
import jax.numpy as jnp
from jax import lax
from jax.experimental import pallas as pl
from jax.experimental.pallas import tpu as pltpu

D_MODEL = 1024
DEPTH = 4
N_META = 16
N_A_LAYERS = DEPTH // 2
D_RNN = 3 * D_MODEL // 2
N_LRU_BLOCKS = 16
LRU_BLOCK = D_RNN // N_LRU_BLOCKS
LRU_C = 8.0
CONV_A_WIDTH = 4
N_FOX_HEADS = 16
FOX_HEAD_DIM = D_MODEL // N_FOX_HEADS
D_FF = 2816
CONV_F_WIDTH = 3
DN_ALPHA = (2 * DEPTH) ** 0.25
LN_EPS = 1e-5

SUBLANES = 8
LANES = 128
ROW_TILE = 688
RNN_GROUP = 4 * LRU_BLOCK
N_RNN_GROUPS = D_RNN // RNN_GROUP
FF_CHUNK = 256
N_FF_CHUNKS = D_FF // FF_CHUNK
ATT_TILE = 256
ATT_UNROLL = 4
HEADS_PER_STEP = LANES // FOX_HEAD_DIM
MASK_VALUE = -1e30
VMEM_LIMIT = 56 * 1024 * 1024

F32 = jnp.float32
BF16 = jnp.bfloat16


def _dot(a, b):
    return jnp.dot(a, b, preferred_element_type=F32)


def _layer_norm_rows(z, g, b):
    mu = jnp.mean(z, axis=-1, keepdims=True)
    zc = z - mu
    var = jnp.mean(zc * zc, axis=-1, keepdims=True)
    return zc * lax.rsqrt(var + LN_EPS) * g + b


def _gelu_tanh(x):
    c = math.sqrt(2.0 / math.pi)
    k1 = -2.0 * c * math.log2(math.e)
    k3 = k1 * 0.044715
    return x / (1.0 + jnp.exp2(x * (k3 * (x * x) + k1)))


def _sigmoid(x):
    return 1.0 / (1.0 + jnp.exp2(x * (-math.log2(math.e))))


def _softplus(x):
    return jnp.maximum(x, 0.0) + jnp.log1p(jnp.exp(-jnp.abs(x)))


def _const_spec(shape):
    return pl.BlockSpec(shape, lambda *_: (0,) * len(shape), pipeline_mode=pl.Buffered(1))


def _row_spec(tl, width):
    return pl.BlockSpec((1, tl, width), lambda b, t: (b, t, 0))


def _params():
    return pltpu.CompilerParams(dimension_semantics=("arbitrary", "arbitrary"),
                                vmem_limit_bytes=VMEM_LIMIT)


def _segment_rows(buf, slab, start, seg):
    return buf[slab, pl.ds(start, SUBLANES, stride=seg), :]


def _causal_conv_interleaved(buf, slab, width, seg, w_ref, b_ref, cols):
    first = SUBLANES - (width - 1)
    cat = jnp.concatenate([_segment_rows(buf, slab, first + q, seg)
                           for q in range(seg + width - 1)], axis=0)
    out = b_ref[:, cols]
    for k in range(width):
        out = out + cat[k * SUBLANES:(k + seg) * SUBLANES, :] * w_ref[k:k + 1, cols]
    return out


def _store_time_order(buf, slab, x, seg):
    for p in range(seg):
        buf[slab, pl.ds(p, SUBLANES, stride=seg), :] = x[p * SUBLANES:(p + 1) * SUBLANES, :]


def _scan_segments(a, u, carry, seg):
    width = a.shape[1]
    h = jnp.zeros((SUBLANES, width), F32)
    prod = jnp.ones((SUBLANES, width), F32)
    local, prods = [], []
    for p in range(seg):
        blk = slice(p * SUBLANES, (p + 1) * SUBLANES)
        h = a[blk] * h + u[blk]
        prod = a[blk] * prod
        local.append(h)
        prods.append(prod)
    row = lax.broadcasted_iota(jnp.int32, (SUBLANES, width), 0)
    h_in = jnp.broadcast_to(carry, (SUBLANES, width))
    for _ in range(SUBLANES - 1):
        h_in = jnp.where(row == 0, carry, pltpu.roll(h + prod * h_in, 1, 0))
    states = [local[p] + prods[p] * h_in for p in range(seg)]
    return states, states[-1][SUBLANES - 1:SUBLANES, :]


def _recurrent_kernel(x_ref, w_in_ref, cw_ref, cb_ref, wg_ref, br_ref, bi_ref, lam_ref,
                      w_out_ref, g_ref, b_ref, o_ref,
                      tail_ref, hc_ref, p_buf, h_buf, y_buf):
    tl = x_ref.shape[1]
    seg = tl // SUBLANES
    slabs = RNN_GROUP // LANES

    @pl.when(pl.program_id(1) == 0)
    def _():
        tail_ref[...] = jnp.zeros_like(tail_ref)
        hc_ref[...] = jnp.zeros_like(hc_ref)

    x = x_ref[0]
    xb = x.astype(BF16)
    for g in range(N_RNN_GROUPS):
        c0, c1 = g * RNN_GROUP, (g + 1) * RNN_GROUP
        gate = _dot(xb, w_in_ref[:, c0:c1])
        rec = _dot(xb, w_in_ref[:, D_RNN + c0:D_RNN + c1])
        tail = tail_ref[:, c0:c1]
        tail_ref[:, c0:c1] = rec[tl - SUBLANES:, :]
        cols = []
        for c in range(slabs):
            lanes = slice(c * LANES, (c + 1) * LANES)
            p_buf[g * slabs + c, 0:SUBLANES, :] = tail[:, lanes]
            p_buf[g * slabs + c, SUBLANES:, :] = rec[:, lanes]
            cols.append(_causal_conv_interleaved(
                p_buf, g * slabs + c, CONV_A_WIDTH, seg, cw_ref, cb_ref,
                slice(c0 + c * LANES, c0 + (c + 1) * LANES)))
        rc = jnp.concatenate(cols, axis=1)
        ri = _dot(rc.astype(BF16), wg_ref[g])
        r = _sigmoid(ri[:, :RNN_GROUP] + br_ref[:, c0:c1])
        i = _sigmoid(ri[:, RNN_GROUP:] + bi_ref[:, c0:c1])
        log_a = (-LRU_C) * r * _softplus(-lam_ref[:, c0:c1])
        a = jnp.exp(log_a)
        t = -jnp.tanh(log_a)
        u = jnp.sqrt(2.0 * t / (1.0 + t)) * (i * rc)
        states, hc_ref[:, c0:c1] = _scan_segments(a, u, hc_ref[:, c0:c1], seg)
        h_all = jnp.concatenate(states, axis=0)
        for c in range(slabs):
            _store_time_order(h_buf, g * slabs + c, h_all[:, c * LANES:(c + 1) * LANES], seg)
        h_time = jnp.concatenate([h_buf[g * slabs + c] for c in range(slabs)], axis=1)
        y_buf[:, c0:c1] = (_gelu_tanh(gate) * h_time).astype(BF16)
    mix = _dot(y_buf[...], w_out_ref[...])
    o_ref[0] = _layer_norm_rows(DN_ALPHA * x + mix, g_ref[...], b_ref[...])


def _recurrent_layer(h, w_in, cw, cb, wg, br, bi, lam, w_out, ln_g, ln_b):
    bsz, length, d = h.shape
    tl = ROW_TILE
    return pl.pallas_call(
        _recurrent_kernel,
        out_shape=jax.ShapeDtypeStruct(h.shape, F32),
        grid=(bsz, length // tl),
        in_specs=[_row_spec(tl, d), _const_spec(w_in.shape), _const_spec(cw.shape),
                  _const_spec(cb.shape), _const_spec(wg.shape), _const_spec(br.shape),
                  _const_spec(bi.shape), _const_spec(lam.shape), _const_spec(w_out.shape),
                  _const_spec(ln_g.shape), _const_spec(ln_b.shape)],
        out_specs=_row_spec(tl, d),
        scratch_shapes=[pltpu.VMEM((SUBLANES, D_RNN), F32),
                        pltpu.VMEM((1, D_RNN), F32),
                        pltpu.VMEM((D_RNN // LANES, tl + SUBLANES, LANES), F32),
                        pltpu.VMEM((D_RNN // LANES, tl, LANES), F32),
                        pltpu.VMEM((tl, D_RNN), BF16)],
        compiler_params=_params(),
        name="recurrent_mixer",
    )(h, w_in, cw, cb, wg, br, bi, lam, w_out, ln_g, ln_b)


def _ffn_kernel(x_ref, w_in_ref, cw_ref, cb_ref, w_out_ref, g_ref, b_ref, o_ref,
                tail_ref, p_buf, y_buf):
    tl = x_ref.shape[1]

    @pl.when(pl.program_id(1) == 0)
    def _():
        tail_ref[...] = jnp.zeros_like(tail_ref)

    x = x_ref[0]
    xb = x.astype(BF16)

    def conv_branch(c0, c1):
        hid = _dot(xb, w_in_ref[:, c0:c1])
        p_buf[0:SUBLANES, :] = tail_ref[:, c0:c1]
        p_buf[SUBLANES:, :] = hid
        tail_ref[:, c0:c1] = hid[tl - SUBLANES:, :]
        out = cb_ref[:, c0:c1]
        for k in range(CONV_F_WIDTH):
            off = SUBLANES - (CONV_F_WIDTH - 1) + k
            out = out + p_buf[off:off + tl, :] * cw_ref[k:k + 1, c0:c1]
        return out

    for c in range(N_FF_CHUNKS):
        c0, c1 = c * FF_CHUNK, (c + 1) * FF_CHUNK
        gate = conv_branch(c0, c1)
        val = conv_branch(D_FF + c0, D_FF + c1)
        y_buf[:, c0:c1] = (_gelu_tanh(gate) * val).astype(BF16)
    ffn = _dot(y_buf[...], w_out_ref[...])
    o_ref[0] = _layer_norm_rows(DN_ALPHA * x + ffn, g_ref[...], b_ref[...])


def _ffn_layer(h, w_in, cw, cb, w_out, ln_g, ln_b):
    bsz, length, d = h.shape
    tl = ROW_TILE
    return pl.pallas_call(
        _ffn_kernel,
        out_shape=jax.ShapeDtypeStruct(h.shape, F32),
        grid=(bsz, length // tl),
        in_specs=[_row_spec(tl, d), _const_spec(w_in.shape), _const_spec(cw.shape),
                  _const_spec(cb.shape), _const_spec(w_out.shape),
                  _const_spec(ln_g.shape), _const_spec(ln_b.shape)],
        out_specs=_row_spec(tl, d),
        scratch_shapes=[pltpu.VMEM((SUBLANES, 2 * D_FF), F32),
                        pltpu.VMEM((tl + SUBLANES, FF_CHUNK), F32),
                        pltpu.VMEM((tl, D_FF), BF16)],
        compiler_params=_params(),
        name="conv_ffn",
    )(h, w_in, cw, cb, w_out, ln_g, ln_b)


def _pair_shape(bsz, length):
    return (bsz, N_FOX_HEADS // HEADS_PER_STEP, length, LANES)


def _pair_row_spec(tl):
    return pl.BlockSpec((1, N_FOX_HEADS // HEADS_PER_STEP, tl, LANES), lambda b, t: (b, 0, t, 0))


def _store_head_pairs(ref, x):
    for hp in range(ref.shape[1]):
        ref[0, hp] = x[:, hp * LANES:(hp + 1) * LANES]


def _load_head_pairs(ref, rows=slice(None)):
    return jnp.concatenate([ref[0, hp, rows, :] for hp in range(ref.shape[1])], axis=1)


def _kv_kernel(x_ref, wk_ref, wv_ref, wf_ref, fb_ref, k_ref, v_ref, c_ref, carry_ref):
    tl = x_ref.shape[1]

    @pl.when(pl.program_id(1) == 0)
    def _():
        carry_ref[...] = jnp.zeros_like(carry_ref)

    xb = x_ref[0].astype(BF16)
    _store_head_pairs(k_ref, _dot(xb, wk_ref[...]).astype(BF16))
    _store_head_pairs(v_ref, _dot(xb, wv_ref[...]).astype(BF16))
    zf = _dot(xb, wf_ref[...]) + fb_ref[...]
    c = jnp.minimum(zf, 0.0) - jnp.log1p(jnp.exp(-jnp.abs(zf)))
    row = lax.broadcasted_iota(jnp.int32, c.shape, 0)
    d = 1
    while d < tl:
        c = c + jnp.where(row >= d, pltpu.roll(c, d, 0), 0.0)
        d *= 2
    c = c + carry_ref[...]
    carry_ref[...] = c[tl - 1:tl, :]
    c_ref[0] = c


def _kv_projection(h, wk, wv, wf, fb):
    bsz, length, d = h.shape
    tl = ROW_TILE
    return pl.pallas_call(
        _kv_kernel,
        out_shape=(jax.ShapeDtypeStruct(_pair_shape(bsz, length), BF16),
                   jax.ShapeDtypeStruct(_pair_shape(bsz, length), BF16),
                   jax.ShapeDtypeStruct((bsz, length, LANES), F32)),
        grid=(bsz, length // tl),
        in_specs=[_row_spec(tl, d), _const_spec(wk.shape), _const_spec(wv.shape),
                  _const_spec(wf.shape), _const_spec(fb.shape)],
        out_specs=(_pair_row_spec(tl), _pair_row_spec(tl), _row_spec(tl, LANES)),
        scratch_shapes=[pltpu.VMEM((1, LANES), F32)],
        compiler_params=_params(),
        name="kv_projection",
    )(h, wk, wv, wf, fb)


def _q_kernel(x_ref, w_ref, q_ref, gate_ref):
    xb = x_ref[0].astype(BF16)
    scale = FOX_HEAD_DIM ** -0.5
    _store_head_pairs(q_ref, (_dot(xb, w_ref[:, :D_MODEL]) * scale).astype(BF16))
    gate_ref[0] = _sigmoid(_dot(xb, w_ref[:, D_MODEL:])).astype(BF16)


def _q_projection(h, w):
    bsz, length, d = h.shape
    tl = ROW_TILE
    return pl.pallas_call(
        _q_kernel,
        out_shape=(jax.ShapeDtypeStruct(_pair_shape(bsz, length), BF16),
                   jax.ShapeDtypeStruct(h.shape, BF16)),
        grid=(bsz, length // tl),
        in_specs=[_row_spec(tl, d), _const_spec(w.shape)],
        out_specs=(_pair_row_spec(tl), _row_spec(tl, d)),
        compiler_params=_params(),
        name="q_projection",
    )(h, w)


def _nt_dot(a, b):
    return lax.dot_general(a, b, (((1,), (1,)), ((), ())), preferred_element_type=F32)


def _attention_kernel(q_ref, k_ref, v_ref, c_ref, o_ref, *scratch):
    t = ATT_TILE
    u = ATT_UNROLL
    s_bufs, p_bufs, a_bufs = scratch[0:u], scratch[u:2 * u], scratch[2 * u:3 * u]
    m_buf, acc_buf, bias_buf, crep_buf, vt_buf = scratch[3 * u:]
    n_tiles = q_ref.shape[0] // t
    n_pairs = n_tiles * (n_tiles + 1) // 2
    assert n_pairs % u == 0
    ones_rows = vt_buf.shape[1] - LANES
    lane = lax.broadcasted_iota(jnp.int32, (1, LANES), 1)
    head0 = lane < FOX_HEAD_DIM

    key =lax.broadcasted_iota(jnp.int32, (t, t), 0)
    qry = lax.broadcasted_iota(jnp.int32, (t, t), 1)
    bias_buf[0] = jnp.zeros((t, t), F32)
    bias_buf[1] = jnp.where(key <= qry, 0.0, MASK_VALUE)
    for jb in range(n_tiles):
        rows = slice(jb * t, (jb + 1) * t)
        vt_buf[jb, 0:LANES, :] = v_ref[rows, :].astype(F32).T.astype(BF16)
        vt_buf[jb, LANES:, :] = jnp.ones((ones_rows, t), BF16)
        for hh in range(HEADS_PER_STEP):
            crep_buf[hh, rows, :] = jnp.broadcast_to(c_ref[0, hh, jb], (LANES, t)).T
    o_ref[n_tiles * t:, :] = jnp.zeros((o_ref.shape[0] - n_tiles * t, LANES), BF16)
    for buf in (*p_bufs, *a_bufs, acc_buf):
        buf[...] = jnp.zeros_like(buf)
    m_buf[...] = jnp.full_like(m_buf, MASK_VALUE)

    def scores(qi, j, s_out):
        q = q_ref[pl.ds(pl.multiple_of(qi * t, t), t), :]
        zero = jnp.zeros_like(q)
        q2 = jnp.concatenate([jnp.where(head0, q, zero), jnp.where(head0, zero, q)], axis=0)
        s_out[...] = _nt_dot(k_ref[pl.ds(pl.multiple_of(j * t, t), t), :], q2)

    def softmax(qi, j, s_in, p_out, a_out):
        masked = (j == qi).astype(jnp.int32)
        keys = pl.ds(pl.multiple_of(j * t, t), t)
        first = j == 0
        for hh in range(HEADS_PER_STEP):
            for half in range(t // LANES):
                cols = slice(hh * t + half * LANES, hh * t + (half + 1) * LANES)
                s = (s_in[:, cols] - crep_buf[hh, keys, :]
                     + bias_buf[masked, :, half * LANES:(half + 1) * LANES])
                m_old = jnp.where(first, MASK_VALUE, m_buf[:, cols])
                m_new = jnp.maximum(m_old, jnp.max(s, axis=0, keepdims=True))
                m_buf[:, cols] = m_new
                a_out[:, cols] = jnp.exp(m_old - m_new)
                p_out[:, cols] = jnp.exp(s - m_new).astype(BF16)

    def accumulate(qi, j, p_in, a_in):
        acc_buf[qi] = acc_buf[qi] * a_in[...] + _dot(vt_buf[j], p_in[...])

    def next_pair(pair):
        qi, j = pair
        last = j == qi
        return jnp.minimum(jnp.where(last, qi + 1, qi), n_tiles - 1), jnp.where(last, 0, j + 1)

    def following(pair, count):
        out = []
        for _ in range(count):
            pair = next_pair(pair)
            out.append(pair)
        return out

    def body(_, carry):
        done, cur, nxt0 = carry[0:u], carry[u:2 * u], carry[2 * u]
        nxt = [nxt0] + following(nxt0, u - 1)
        for i in range(u):
            accumulate(*done[i], p_bufs[i], a_bufs[i])
        for i in range(u):
            softmax(*cur[i], s_bufs[i], p_bufs[i], a_bufs[i])
        for i in range(u):
            scores(*nxt[i], s_bufs[i])
        return (*cur, *nxt, next_pair(nxt[-1]))

    zero = jnp.int32(0)
    empty = (zero, zero + 1)
    first = [(zero, zero)] + following((zero, zero), u)
    for i in range(u):
        scores(*first[i], s_bufs[i])
    carry = lax.fori_loop(0, n_pairs // u, body, (*([empty] * u), *first))
    for i in range(u):
        accumulate(*carry[i], p_bufs[i], a_bufs[i])

    head0_rows = lax.broadcasted_iota(jnp.int32, (LANES, 1), 0) < FOX_HEAD_DIM
    for qi in range(n_tiles):
        o2 = acc_buf[qi, 0:LANES, :] / acc_buf[qi, LANES:LANES + 1, :]
        o_ref[qi * t:(qi + 1) * t, :] = (
            jnp.where(head0_rows, o2[:, 0:t], o2[:, t:]).T.astype(BF16))


def _attention_tail_kernel(q_ref, k_ref, v_ref, c_ref, o_ref, m_buf, l_buf, acc_buf):
    t = ATT_TILE
    rem = q_ref.shape[2]
    length = k_ref.shape[2]
    n_full = (length - rem) // t
    rows = N_FOX_HEADS * rem
    lane = lax.broadcasted_iota(jnp.int32, (1, D_MODEL), 1)
    head_lanes = [(lane >= h * FOX_HEAD_DIM) & (lane < (h + 1) * FOX_HEAD_DIM)
                  for h in range(N_FOX_HEADS)]
    q = _load_head_pairs(q_ref)
    qs = jnp.concatenate([jnp.where(hl, q, jnp.zeros_like(q)) for hl in head_lanes], axis=0)
    m_buf[...] = jnp.full_like(m_buf, MASK_VALUE)
    l_buf[...] = jnp.zeros_like(l_buf)
    acc_buf[...] = jnp.zeros_like(acc_buf)

    def update(k_blk, v_blk, c_stack, mask):
        s = _nt_dot(qs, k_blk) - c_stack
        if mask is not None:
            s = jnp.where(mask, s, MASK_VALUE)
        m_old = m_buf[...]
        m_new = jnp.maximum(m_old, jnp.max(s, axis=1, keepdims=True))
        alpha = jnp.exp(m_old - m_new)
        p = jnp.exp(s - m_new)
        m_buf[...] = m_new
        l_buf[...] = alpha * l_buf[...] + jnp.sum(p, axis=1, keepdims=True)
        acc_buf[...] = alpha * acc_buf[...] + _dot(p.astype(BF16), v_blk)

    def c_stacked(j, width):
        return jnp.concatenate([jnp.broadcast_to(c_ref[0, h, j][:, :width], (rem, width))
                                for h in range(N_FOX_HEADS)], axis=0)

    def body(j, carry):
        keys = pl.ds(pl.multiple_of(j * t, t), t)
        update(_load_head_pairs(k_ref, keys), _load_head_pairs(v_ref, keys), c_stacked(j, t), None)
        return carry

    lax.fori_loop(0, n_full, body, 0)
    r = lax.broadcasted_iota(jnp.int32, (rows, rem), 0) % rem
    cc = lax.broadcasted_iota(jnp.int32, (rows, rem), 1)
    last_keys = slice(n_full * t, length)
    update(_load_head_pairs(k_ref, last_keys), _load_head_pairs(v_ref, last_keys),
           c_stacked(n_full, rem), cc <= r)
    o_all = acc_buf[...] / l_buf[...]
    o = o_all[0:rem]
    for h in range(1, N_FOX_HEADS):
        o = jnp.where(head_lanes[h], o_all[h * rem:(h + 1) * rem], o)
    o_ref[0] = o.astype(BF16)


def _attention(q, k, v, c_rows):
    bsz, n_hp, length, _ = q.shape
    d = D_MODEL
    t = ATT_TILE
    n_blk = c_rows.shape[2]
    main = (length // t) * t
    rem = length - main
    qcols = HEADS_PER_STEP * t
    col_spec = pl.BlockSpec((None, None, length, LANES), lambda b, hp: (b, hp, 0, 0))
    o = pl.pallas_call(
        _attention_kernel,
        out_shape=jax.ShapeDtypeStruct(q.shape, BF16),
        grid=(bsz, N_FOX_HEADS // HEADS_PER_STEP),
        in_specs=[col_spec, col_spec, col_spec,
                  pl.BlockSpec((1, HEADS_PER_STEP, n_blk, 1, t), lambda b, hp: (b, hp, 0, 0, 0))],
        out_specs=col_spec,
        scratch_shapes=[*[pltpu.VMEM((t, qcols), F32)] * ATT_UNROLL,
                        *[pltpu.VMEM((t, qcols), BF16)] * ATT_UNROLL,
                        *[pltpu.VMEM((1, qcols), F32)] * ATT_UNROLL,
                        pltpu.VMEM((1, qcols), F32),
                        pltpu.VMEM((main // t, LANES + 2 * SUBLANES, qcols), F32),
                        pltpu.VMEM((2, t, t), F32),
                        pltpu.VMEM((HEADS_PER_STEP, main, LANES), F32),
                        pltpu.VMEM((main // t, LANES + 2 * SUBLANES, t), BF16)],
        compiler_params=_params(),
        name="fox_attention",
    )(q, k, v, c_rows)
    assert rem and main % rem == 0
    full_spec = pl.BlockSpec((1, n_hp, length, LANES), lambda b: (b, 0, 0, 0))
    o_tail = pl.pallas_call(
        _attention_tail_kernel,
        out_shape=jax.ShapeDtypeStruct((bsz, rem, d), BF16),
        grid=(bsz,),
        in_specs=[pl.BlockSpec((1, n_hp, rem, LANES), lambda b: (b, 0, main // rem, 0)),
                  full_spec, full_spec,
                  pl.BlockSpec((1, N_FOX_HEADS, n_blk, 1, t), lambda b: (b, 0, 0, 0, 0))],
        out_specs=pl.BlockSpec((1, rem, d), lambda b: (b, 0, 0)),
        scratch_shapes=[pltpu.VMEM((N_FOX_HEADS * rem, 1), F32),
                        pltpu.VMEM((N_FOX_HEADS * rem, 1), F32),
                        pltpu.VMEM((N_FOX_HEADS * rem, d), F32)],
        compiler_params=pltpu.CompilerParams(dimension_semantics=("arbitrary",),
                                             vmem_limit_bytes=VMEM_LIMIT),
        name="fox_attention_tail",
    )(q, k, v, c_rows)
    return o, o_tail


def _out_kernel(x_ref, o_ref, o_tail_ref, gate_ref, w_ref, g_ref, b_ref, y_ref):
    tl = x_ref.shape[1]
    rem = o_tail_ref.shape[1]
    o = _load_head_pairs(o_ref)
    is_last = pl.program_id(1) == pl.num_programs(1) - 1
    o = jnp.where(is_last, jnp.concatenate([o[:tl - rem], o_tail_ref[0]], axis=0), o)
    mix = _dot(o * gate_ref[0], w_ref[...])
    y_ref[0] = _layer_norm_rows(DN_ALPHA * x_ref[0] + mix, g_ref[...], b_ref[...])


def _out_projection(h, o, o_tail, gate, w, ln_g, ln_b):
    bsz, length, d = h.shape
    tl = ROW_TILE
    return pl.pallas_call(
        _out_kernel,
        out_shape=jax.ShapeDtypeStruct(h.shape, F32),
        grid=(bsz, length // tl),
        in_specs=[_row_spec(tl, d), _pair_row_spec(tl),
                  pl.BlockSpec((1,) + o_tail.shape[1:], lambda b, t: (b, 0, 0)),
                  _row_spec(tl, d), _const_spec(w.shape),
                  _const_spec(ln_g.shape), _const_spec(ln_b.shape)],
        out_specs=_row_spec(tl, d),
        compiler_params=_params(),
        name="attn_out_projection",
    )(h, o, o_tail, gate, w, ln_g, ln_b)


def _gate_weights(w_r, w_i):
    per = RNN_GROUP // LRU_BLOCK
    eye = jnp.eye(per, dtype=w_r.dtype)

    def block_diag(w):
        w = w.reshape(N_RNN_GROUPS, per, LRU_BLOCK, LRU_BLOCK)
        return jnp.einsum('gacd,ab->gacbd', w, eye).reshape(N_RNN_GROUPS, RNN_GROUP, RNN_GROUP)

    return jnp.concatenate([block_diag(w_r), block_diag(w_i)], axis=-1).astype(BF16)


def _row(v):
    return v.reshape(1, -1).astype(F32)


def kernel(x, meta, a_w_in, a_conv_w, a_conv_b, a_w_r, a_b_r, a_w_i, a_b_i, a_lambda, a_w_out, kv_w, kv_f_b, b_w_in, b_w_out, f_w_in, f_conv_w, f_conv_b, f_w_out, ln1_g, ln1_b, ln2_g, ln2_b):
    bsz, seq, d = x.shape
    length = seq + N_META
    assert d == D_MODEL and length % ROW_TILE == 0
    h = jnp.concatenate([jnp.broadcast_to(meta.astype(x.dtype), (bsz, N_META, d)), x], axis=1)

    k = v = c_rows = None
    for layer in range(DEPTH):
        if layer < N_A_LAYERS:
            h = _recurrent_layer(
                h, a_w_in[layer].astype(BF16), a_conv_w[layer], _row(a_conv_b[layer]),
                _gate_weights(a_w_r[layer], a_w_i[layer]), _row(a_b_r[layer]), _row(a_b_i[layer]),
                _row(a_lambda[layer]), a_w_out[layer].astype(BF16),
                _row(ln1_g[layer]), _row(ln1_b[layer]))
        else:
            if layer == N_A_LAYERS:
                wf = jnp.pad(kv_w[:, 2 * d:], ((0, 0), (0, LANES - N_FOX_HEADS))).astype(BF16)
                fb = jnp.pad(kv_f_b, (0, LANES - N_FOX_HEADS)).reshape(1, LANES).astype(F32)
                k, v, c_cols = _kv_projection(h, kv_w[:, :d].astype(BF16),
                                              kv_w[:, d:2 * d].astype(BF16), wf, fb)
                n_blk = -(-length // ATT_TILE)
                c_rows = jnp.transpose(c_cols[:, :, :N_FOX_HEADS], (0, 2, 1))
                c_rows = jnp.pad(c_rows, ((0, 0), (0, 0), (0, n_blk * ATT_TILE - length)))
                c_rows = c_rows.reshape(bsz, N_FOX_HEADS, n_blk, 1, ATT_TILE)
            j = layer - N_A_LAYERS
            q, gate = _q_projection(h, b_w_in[j].astype(BF16))
            o, o_tail = _attention(q, k, v, c_rows)
            h = _out_projection(h, o, o_tail, gate, b_w_out[j].astype(BF16),
                                _row(ln1_g[layer]), _row(ln1_b[layer]))
        h = _ffn_layer(h, f_w_in[layer].astype(BF16), f_conv_w[layer], _row(f_conv_b[layer]),
                       f_w_out[layer].astype(BF16), _row(ln2_g[layer]), _row(ln2_b[layer]))
    return h[:, N_META:]
```

```python
import functools
import math

import jax
import jax.numpy as jnp
from jax import lax
from jax.experimental import pallas as pl
from jax.experimental.pallas import tpu as pltpu

D_MODEL = 1024
DEPTH = 4
N_META = 16
N_A_LAYERS = DEPTH // 2
D_RNN = 3 * D_MODEL // 2
N_LRU_BLOCKS = 16
LRU_BLOCK = D_RNN // N_LRU_BLOCKS
LRU_C = 8.0
CONV_A_WIDTH = 4
N_FOX_HEADS = 16
FOX_HEAD_DIM = D_MODEL // N_FOX_HEADS
D_FF = 2816
CONV_F_WIDTH = 3
DN_ALPHA = (2 * DEPTH) ** 0.25
LN_EPS = 1e-5

SUBLANES = 8
LANES = 128
ROW_TILE = 688
RNN_GROUP = 4 * LRU_BLOCK
N_RNN_GROUPS = D_RNN // RNN_GROUP
FF_CHUNK = 256
N_FF_CHUNKS = D_FF // FF_CHUNK
ATT_TILE = 256
ATT_UNROLL = 4
HEADS_PER_STEP = LANES // FOX_HEAD_DIM
MASK_VALUE = -1e30
VMEM_LIMIT = 56 * 1024 * 1024

F32 = jnp.float32
BF16 = jnp.bfloat16


def _dot(a, b):
    return jnp.dot(a, b, preferred_element_type=F32)


def _layer_norm_rows(z, g, b):
    mu = jnp.mean(z, axis=-1, keepdims=True)
    zc = z - mu
    var = jnp.mean(zc * zc, axis=-1, keepdims=True)
    return zc * lax.rsqrt(var + LN_EPS) * g + b


def _gelu_tanh(x):
    c = math.sqrt(2.0 / math.pi)
    return 0.5 * x * (1.0 + jnp.tanh(c * (x + 0.044715 * (x * x * x))))


def _sigmoid(x):
    return 1.0 / (1.0 + jnp.exp2(x * (-math.log2(math.e))))


def _softplus(x):
    return jnp.maximum(x, 0.0) + jnp.log1p(jnp.exp(-jnp.abs(x)))


def _const_spec(shape):
    return pl.BlockSpec(shape, lambda *_: (0,) * len(shape), pipeline_mode=pl.Buffered(1))


def _row_spec(tl, width):
    return pl.BlockSpec((1, tl, width), lambda b, t: (b, t, 0))


def _params():
    return pltpu.CompilerParams(dimension_semantics=("arbitrary", "arbitrary"),
                                vmem_limit_bytes=VMEM_LIMIT)


def _segment_rows(buf, slab, start, seg):
    return buf[slab, pl.ds(start, SUBLANES, stride=seg), :]


def _causal_conv_interleaved(buf, slab, width, seg, w_ref, b_ref, cols):
    first = SUBLANES - (width - 1)
    cat = jnp.concatenate([_segment_rows(buf, slab, first + q, seg)
                           for q in range(seg + width - 1)], axis=0)
    out = b_ref[:, cols]
    for k in range(width):
        out = out + cat[k * SUBLANES:(k + seg) * SUBLANES, :] * w_ref[k:k + 1, cols]
    return out


def _store_time_order(buf, slab, x, seg):
    for p in range(seg):
        buf[slab, pl.ds(p, SUBLANES, stride=seg), :] = x[p * SUBLANES:(p + 1) * SUBLANES, :]


def _scan_segments(a, u, carry, seg):
    width = a.shape[1]
    h = jnp.zeros((SUBLANES, width), F32)
    prod = jnp.ones((SUBLANES, width), F32)
    local, prods = [], []
    for p in range(seg):
        blk = slice(p * SUBLANES, (p + 1) * SUBLANES)
        h = a[blk] * h + u[blk]
        prod = a[blk] * prod
        local.append(h)
        prods.append(prod)
    row = lax.broadcasted_iota(jnp.int32, (SUBLANES, width), 0)
    h_in = jnp.broadcast_to(carry, (SUBLANES, width))
    for _ in range(SUBLANES - 1):
        h_in = jnp.where(row == 0, carry, pltpu.roll(h + prod * h_in, 1, 0))
    states = [local[p] + prods[p] * h_in for p in range(seg)]
    return states, states[-1][SUBLANES - 1:SUBLANES, :]


def _recurrent_kernel(x_ref, w_in_ref, cw_ref, cb_ref, wg_ref, br_ref, bi_ref, lam_ref,
                      w_out_ref, g_ref, b_ref, o_ref,
                      tail_ref, hc_ref, p_buf, h_buf, y_buf):
    tl = x_ref.shape[1]
    seg = tl // SUBLANES
    slabs = RNN_GROUP // LANES

    @pl.when(pl.program_id(1) == 0)
    def _():
        tail_ref[...] = jnp.zeros_like(tail_ref)
        hc_ref[...] = jnp.zeros_like(hc_ref)

    x = x_ref[0]
    xb = x.astype(BF16)
    for g in range(N_RNN_GROUPS):
        c0, c1 = g * RNN_GROUP, (g + 1) * RNN_GROUP
        gate = _dot(xb, w_in_ref[:, c0:c1])
        rec = _dot(xb, w_in_ref[:, D_RNN + c0:D_RNN + c1])
        tail = tail_ref[:, c0:c1]
        tail_ref[:, c0:c1] = rec[tl - SUBLANES:, :]
        cols = []
        for c in range(slabs):
            lanes = slice(c * LANES, (c + 1) * LANES)
            p_buf[g * slabs + c, 0:SUBLANES, :] = tail[:, lanes]
            p_buf[g * slabs + c, SUBLANES:, :] = rec[:, lanes]
            cols.append(_causal_conv_interleaved(
                p_buf, g * slabs + c, CONV_A_WIDTH, seg, cw_ref, cb_ref,
                slice(c0 + c * LANES, c0 + (c + 1) * LANES)))
        rc = jnp.concatenate(cols, axis=1)
        ri = _dot(rc.astype(BF16), wg_ref[g])
        r = _sigmoid(ri[:, :RNN_GROUP] + br_ref[:, c0:c1])
        i = _sigmoid(ri[:, RNN_GROUP:] + bi_ref[:, c0:c1])
        log_a = (-LRU_C) * r * _softplus(-lam_ref[:, c0:c1])
        a = jnp.exp(log_a)
        t = -jnp.tanh(log_a)
        u = jnp.sqrt(2.0 * t / (1.0 + t)) * (i * rc)
        states, hc_ref[:, c0:c1] = _scan_segments(a, u, hc_ref[:, c0:c1], seg)
        h_all = jnp.concatenate(states, axis=0)
        for c in range(slabs):
            _store_time_order(h_buf, g * slabs + c, h_all[:, c * LANES:(c + 1) * LANES], seg)
        h_time = jnp.concatenate([h_buf[g * slabs + c] for c in range(slabs)], axis=1)
        y_buf[:, c0:c1] = (_gelu_tanh(gate) * h_time).astype(BF16)
    mix = _dot(y_buf[...], w_out_ref[...])
    o_ref[0] = _layer_norm_rows(DN_ALPHA * x + mix, g_ref[...], b_ref[...])


def _recurrent_layer(h, w_in, cw, cb, wg, br, bi, lam, w_out, ln_g, ln_b):
    bsz, length, d = h.shape
    tl = ROW_TILE
    return pl.pallas_call(
        _recurrent_kernel,
        out_shape=jax.ShapeDtypeStruct(h.shape, F32),
        grid=(bsz, length // tl),
        in_specs=[_row_spec(tl, d), _const_spec(w_in.shape), _const_spec(cw.shape),
                  _const_spec(cb.shape), _const_spec(wg.shape), _const_spec(br.shape),
                  _const_spec(bi.shape), _const_spec(lam.shape), _const_spec(w_out.shape),
                  _const_spec(ln_g.shape), _const_spec(ln_b.shape)],
        out_specs=_row_spec(tl, d),
        scratch_shapes=[pltpu.VMEM((SUBLANES, D_RNN), F32),
                        pltpu.VMEM((1, D_RNN), F32),
                        pltpu.VMEM((D_RNN // LANES, tl + SUBLANES, LANES), F32),
                        pltpu.VMEM((D_RNN // LANES, tl, LANES), F32),
                        pltpu.VMEM((tl, D_RNN), BF16)],
        compiler_params=_params(),
        name="recurrent_mixer",
    )(h, w_in, cw, cb, wg, br, bi, lam, w_out, ln_g, ln_b)


def _ffn_kernel(x_ref, w_in_ref, cw_ref, cb_ref, w_out_ref, g_ref, b_ref, o_ref,
                tail_ref, p_buf, y_buf):
    tl = x_ref.shape[1]

    @pl.when(pl.program_id(1) == 0)
    def _():
        tail_ref[...] = jnp.zeros_like(tail_ref)

    x = x_ref[0]
    xb = x.astype(BF16)

    def conv_branch(c0, c1):
        hid = _dot(xb, w_in_ref[:, c0:c1])
        p_buf[0:SUBLANES, :] = tail_ref[:, c0:c1]
        p_buf[SUBLANES:, :] = hid
        tail_ref[:, c0:c1] = hid[tl - SUBLANES:, :]
        out = cb_ref[:, c0:c1]
        for k in range(CONV_F_WIDTH):
            off = SUBLANES - (CONV_F_WIDTH - 1) + k
            out = out + p_buf[off:off + tl, :] * cw_ref[k:k + 1, c0:c1]
        return out

    for c in range(N_FF_CHUNKS):
        c0, c1 = c * FF_CHUNK, (c + 1) * FF_CHUNK
        gate = conv_branch(c0, c1)
        val = conv_branch(D_FF + c0, D_FF + c1)
        y_buf[:, c0:c1] = (_gelu_tanh(gate) * val).astype(BF16)
    ffn = _dot(y_buf[...], w_out_ref[...])
    o_ref[0] = _layer_norm_rows(DN_ALPHA * x + ffn, g_ref[...], b_ref[...])


def _ffn_layer(h, w_in, cw, cb, w_out, ln_g, ln_b):
    bsz, length, d = h.shape
    tl = ROW_TILE
    return pl.pallas_call(
        _ffn_kernel,
        out_shape=jax.ShapeDtypeStruct(h.shape, F32),
        grid=(bsz, length // tl),
        in_specs=[_row_spec(tl, d), _const_spec(w_in.shape), _const_spec(cw.shape),
                  _const_spec(cb.shape), _const_spec(w_out.shape),
                  _const_spec(ln_g.shape), _const_spec(ln_b.shape)],
        out_specs=_row_spec(tl, d),
        scratch_shapes=[pltpu.VMEM((SUBLANES, 2 * D_FF), F32),
                        pltpu.VMEM((tl + SUBLANES, FF_CHUNK), F32),
                        pltpu.VMEM((tl, D_FF), BF16)],
        compiler_params=_params(),
        name="conv_ffn",
    )(h, w_in, cw, cb, w_out, ln_g, ln_b)


def _pair_shape(bsz, length):
    return (bsz, N_FOX_HEADS // HEADS_PER_STEP, length, LANES)


def _pair_row_spec(tl):
    return pl.BlockSpec((1, N_FOX_HEADS // HEADS_PER_STEP, tl, LANES), lambda b, t: (b, 0, t, 0))


def _store_head_pairs(ref, x):
    for hp in range(ref.shape[1]):
        ref[0, hp] = x[:, hp * LANES:(hp + 1) * LANES]


def _load_head_pairs(ref, rows=slice(None)):
    return jnp.concatenate([ref[0, hp, rows, :] for hp in range(ref.shape[1])], axis=1)


def _kv_kernel(x_ref, wk_ref, wv_ref, wf_ref, fb_ref, k_ref, v_ref, c_ref, carry_ref):
    tl = x_ref.shape[1]

    @pl.when(pl.program_id(1) == 0)
    def _():
        carry_ref[...] = jnp.zeros_like(carry_ref)

    xb = x_ref[0].astype(BF16)
    _store_head_pairs(k_ref, _dot(xb, wk_ref[...]).astype(BF16))
    _store_head_pairs(v_ref, _dot(xb, wv_ref[...]).astype(BF16))
    zf = _dot(xb, wf_ref[...]) + fb_ref[...]
    c = jnp.minimum(zf, 0.0) - jnp.log1p(jnp.exp(-jnp.abs(zf)))
    row = lax.broadcasted_iota(jnp.int32, c.shape, 0)
    d = 1
    while d < tl:
        c = c + jnp.where(row >= d, pltpu.roll(c, d, 0), 0.0)
        d *= 2
    c = c + carry_ref[...]
    carry_ref[...] = c[tl - 1:tl, :]
    c_ref[0] = c


def _kv_projection(h, wk, wv, wf, fb):
    bsz, length, d = h.shape
    tl = ROW_TILE
    return pl.pallas_call(
        _kv_kernel,
        out_shape=(jax.ShapeDtypeStruct(_pair_shape(bsz, length), BF16),
                   jax.ShapeDtypeStruct(_pair_shape(bsz, length), BF16),
                   jax.ShapeDtypeStruct((bsz, length, LANES), F32)),
        grid=(bsz, length // tl),
        in_specs=[_row_spec(tl, d), _const_spec(wk.shape), _const_spec(wv.shape),
                  _const_spec(wf.shape), _const_spec(fb.shape)],
        out_specs=(_pair_row_spec(tl), _pair_row_spec(tl), _row_spec(tl, LANES)),
        scratch_shapes=[pltpu.VMEM((1, LANES), F32)],
        compiler_params=_params(),
        name="kv_projection",
    )(h, wk, wv, wf, fb)


def _q_kernel(x_ref, w_ref, q_ref, gate_ref):
    xb = x_ref[0].astype(BF16)
    scale = FOX_HEAD_DIM ** -0.5
    _store_head_pairs(q_ref, (_dot(xb, w_ref[:, :D_MODEL]) * scale).astype(BF16))
    gate_ref[0] = _sigmoid(_dot(xb, w_ref[:, D_MODEL:])).astype(BF16)


def _q_projection(h, w):
    bsz, length, d = h.shape
    tl = ROW_TILE
    return pl.pallas_call(
        _q_kernel,
        out_shape=(jax.ShapeDtypeStruct(_pair_shape(bsz, length), BF16),
                   jax.ShapeDtypeStruct(h.shape, BF16)),
        grid=(bsz, length // tl),
        in_specs=[_row_spec(tl, d), _const_spec(w.shape)],
        out_specs=(_pair_row_spec(tl), _row_spec(tl, d)),
        compiler_params=_params(),
        name="q_projection",
    )(h, w)


def _nt_dot(a, b):
    return lax.dot_general(a, b, (((1,), (1,)), ((), ())), preferred_element_type=F32)


def _attention_kernel(q_ref, k_ref, v_ref, c_ref, o_ref, *scratch):
    t = ATT_TILE
    u = ATT_UNROLL
    s_bufs, p_bufs, a_bufs = scratch[0:u], scratch[u:2 * u], scratch[2 * u:3 * u]
    m_buf, acc_buf, bias_buf, crep_buf, vt_buf, qt_buf = scratch[3 * u:]
    n_tiles = q_ref.shape[0] // t
    n_pairs = n_tiles * (n_tiles + 1) // 2
    assert n_pairs % u == 0
    ones_rows = vt_buf.shape[1] - LANES

    key =lax.broadcasted_iota(jnp.int32, (t, t), 0)
    qry = lax.broadcasted_iota(jnp.int32, (t, t), 1)
    bias_buf[0] = jnp.zeros((t, t), F32)
    bias_buf[1] = jnp.where(key <= qry, 0.0, MASK_VALUE)
    head0_rows = lax.broadcasted_iota(jnp.int32, (LANES, 1), 0) < FOX_HEAD_DIM
    for jb in range(n_tiles):
        rows = slice(jb * t, (jb + 1) * t)
        q_t = q_ref[rows, :].astype(F32).T
        qt_buf[jb] = jnp.concatenate([jnp.where(head0_rows, q_t, 0.0),
                                      jnp.where(head0_rows, 0.0, q_t)], axis=1).astype(BF16)
        vt_buf[jb, 0:LANES, :] = v_ref[rows, :].astype(F32).T.astype(BF16)
        vt_buf[jb, LANES:, :] = jnp.ones((ones_rows, t), BF16)
        for hh in range(HEADS_PER_STEP):
            crep_buf[hh, rows, :] = jnp.broadcast_to(c_ref[0, hh, jb], (LANES, t)).T
    o_ref[n_tiles * t:, :] = jnp.zeros((o_ref.shape[0] - n_tiles * t, LANES), BF16)
    for buf in (*p_bufs, *a_bufs, acc_buf):
        buf[...] = jnp.zeros_like(buf)
    m_buf[...] = jnp.full_like(m_buf, MASK_VALUE)

    def scores(qi, j, s_out):
        s_out[...] = _dot(k_ref[pl.ds(pl.multiple_of(j * t, t), t), :], qt_buf[qi])

    def softmax(qi, j, s_in, p_out, a_out):
        masked = (j == qi).astype(jnp.int32)
        keys = pl.ds(pl.multiple_of(j * t, t), t)
        first = j == 0
        for hh in range(HEADS_PER_STEP):
            for half in range(t // LANES):
                cols = slice(hh * t + half * LANES, hh * t + (half + 1) * LANES)
                s = (s_in[:, cols] - crep_buf[hh, keys, :]
                     + bias_buf[masked, :, half * LANES:(half + 1) * LANES])
                m_old = jnp.where(first, MASK_VALUE, m_buf[:, cols])
                m_new = jnp.maximum(m_old, jnp.max(s, axis=0, keepdims=True))
                m_buf[:, cols] = m_new
                a_out[:, cols] = jnp.exp(m_old - m_new)
                p_out[:, cols] = jnp.exp(s - m_new).astype(BF16)

    def accumulate(qi, j, p_in, a_in):
        acc_buf[qi] = acc_buf[qi] * a_in[...] + _dot(vt_buf[j], p_in[...])

    def next_pair(pair):
        qi, j = pair
        last = j == qi
        return jnp.minimum(jnp.where(last, qi + 1, qi), n_tiles - 1), jnp.where(last, 0, j + 1)

    def following(pair, count):
        out = []
        for _ in range(count):
            pair = next_pair(pair)
            out.append(pair)
        return out

    def body(_, carry):
        done, cur, nxt0 = carry[0:u], carry[u:2 * u], carry[2 * u]
        nxt = [nxt0] + following(nxt0, u - 1)
        for i in range(u):
            accumulate(*done[i], p_bufs[i], a_bufs[i])
        for i in range(u):
            softmax(*cur[i], s_bufs[i], p_bufs[i], a_bufs[i])
        for i in range(u):
            scores(*nxt[i], s_bufs[i])
        return (*cur, *nxt, next_pair(nxt[-1]))

    zero = jnp.int32(0)
    empty = (zero, zero + 1)
    first = [(zero, zero)] + following((zero, zero), u)
    for i in range(u):
        scores(*first[i], s_bufs[i])
    carry = lax.fori_loop(0, n_pairs // u, body, (*([empty] * u), *first))
    for i in range(u):
        accumulate(*carry[i], p_bufs[i], a_bufs[i])

    for qi in range(n_tiles):
        o2 =acc_buf[qi, 0:LANES, :] / acc_buf[qi, LANES:LANES + 1, :]
        o_ref[qi * t:(qi + 1) * t, :] = (
            jnp.where(head0_rows, o2[:, 0:t], o2[:, t:]).T.astype(BF16))


def _attention_tail_kernel(q_ref, k_ref, v_ref, c_ref, o_ref, m_buf, l_buf, acc_buf):
    t = ATT_TILE
    rem = q_ref.shape[2]
    length = k_ref.shape[2]
    n_full = (length - rem) // t
    rows = N_FOX_HEADS * rem
    lane = lax.broadcasted_iota(jnp.int32, (1, D_MODEL), 1)
    head_lanes = [(lane >= h * FOX_HEAD_DIM) & (lane < (h + 1) * FOX_HEAD_DIM)
                  for h in range(N_FOX_HEADS)]
    q = _load_head_pairs(q_ref)
    qs = jnp.concatenate([jnp.where(hl, q, jnp.zeros_like(q)) for hl in head_lanes], axis=0)
    m_buf[...] = jnp.full_like(m_buf, MASK_VALUE)
    l_buf[...] = jnp.zeros_like(l_buf)
    acc_buf[...] = jnp.zeros_like(acc_buf)

    def update(k_blk, v_blk, c_stack, mask):
        s = _nt_dot(qs, k_blk) - c_stack
        if mask is not None:
            s = jnp.where(mask, s, MASK_VALUE)
        m_old = m_buf[...]
        m_new = jnp.maximum(m_old, jnp.max(s, axis=1, keepdims=True))
        alpha = jnp.exp(m_old - m_new)
        p = jnp.exp(s - m_new)
        m_buf[...] = m_new
        l_buf[...] = alpha * l_buf[...] + jnp.sum(p, axis=1, keepdims=True)
        acc_buf[...] = alpha * acc_buf[...] + _dot(p.astype(BF16), v_blk)

    def c_stacked(j, width):
        return jnp.concatenate([jnp.broadcast_to(c_ref[0, h, j][:, :width], (rem, width))
                                for h in range(N_FOX_HEADS)], axis=0)

    def body(j, carry):
        keys = pl.ds(pl.multiple_of(j * t, t), t)
        update(_load_head_pairs(k_ref, keys), _load_head_pairs(v_ref, keys), c_stacked(j, t), None)
        return carry

    lax.fori_loop(0, n_full, body, 0)
    r = lax.broadcasted_iota(jnp.int32, (rows, rem), 0) % rem
    cc = lax.broadcasted_iota(jnp.int32, (rows, rem), 1)
    last_keys = slice(n_full * t, length)
    update(_load_head_pairs(k_ref, last_keys), _load_head_pairs(v_ref, last_keys),
           c_stacked(n_full, rem), cc <= r)
    o_all = acc_buf[...] / l_buf[...]
    o = o_all[0:rem]
    for h in range(1, N_FOX_HEADS):
        o = jnp.where(head_lanes[h], o_all[h * rem:(h + 1) * rem], o)
    o_ref[0] = o.astype(BF16)


def _attention(q, k, v, c_rows):
    bsz, n_hp, length, _ = q.shape
    d = D_MODEL
    t = ATT_TILE
    n_blk = c_rows.shape[2]
    main = (length // t) * t
    rem = length - main
    qcols = HEADS_PER_STEP * t
    col_spec = pl.BlockSpec((None, None, length, LANES), lambda b, hp: (b, hp, 0, 0))
    o = pl.pallas_call(
        _attention_kernel,
        out_shape=jax.ShapeDtypeStruct(q.shape, BF16),
        grid=(bsz, N_FOX_HEADS // HEADS_PER_STEP),
        in_specs=[col_spec, col_spec, col_spec,
                  pl.BlockSpec((1, HEADS_PER_STEP, n_blk, 1, t), lambda b, hp: (b, hp, 0, 0, 0))],
        out_specs=col_spec,
        scratch_shapes=[*[pltpu.VMEM((t, qcols), F32)] * ATT_UNROLL,
                        *[pltpu.VMEM((t, qcols), BF16)] * ATT_UNROLL,
                        *[pltpu.VMEM((1, qcols), F32)] * ATT_UNROLL,
                        pltpu.VMEM((1, qcols), F32),
                        pltpu.VMEM((main // t, LANES + 2 * SUBLANES, qcols), F32),
                        pltpu.VMEM((2, t, t), F32),
                        pltpu.VMEM((HEADS_PER_STEP, main, LANES), F32),
                        pltpu.VMEM((main // t, LANES + 2 * SUBLANES, t), BF16),
                        pltpu.VMEM((main // t, LANES, qcols), BF16)],
        compiler_params=_params(),
        name="fox_attention",
    )(q, k, v, c_rows)
    assert rem and main % rem == 0
    full_spec = pl.BlockSpec((1, n_hp, length, LANES), lambda b: (b, 0, 0, 0))
    o_tail = pl.pallas_call(
        _attention_tail_kernel,
        out_shape=jax.ShapeDtypeStruct((bsz, rem, d), BF16),
        grid=(bsz,),
        in_specs=[pl.BlockSpec((1, n_hp, rem, LANES), lambda b: (b, 0, main // rem, 0)),
                  full_spec, full_spec,
                  pl.BlockSpec((1, N_FOX_HEADS, n_blk, 1, t), lambda b: (b, 0, 0, 0, 0))],
        out_specs=pl.BlockSpec((1, rem, d), lambda b: (b, 0, 0)),
        scratch_shapes=[pltpu.VMEM((N_FOX_HEADS * rem, 1), F32),
                        pltpu.VMEM((N_FOX_HEADS * rem, 1), F32),
                        pltpu.VMEM((N_FOX_HEADS * rem, d), F32)],
        compiler_params=pltpu.CompilerParams(dimension_semantics=("arbitrary",),
                                             vmem_limit_bytes=VMEM_LIMIT),
        name="fox_attention_tail",
    )(q, k, v, c_rows)
    return o, o_tail


def _out_kernel(x_ref, o_ref, o_tail_ref, gate_ref, w_ref, g_ref, b_ref, y_ref):
    tl = x_ref.shape[1]
    rem = o_tail_ref.shape[1]
    o = _load_head_pairs(o_ref)
    is_last = pl.program_id(1) == pl.num_programs(1) - 1
    o = jnp.where(is_last, jnp.concatenate([o[:tl - rem], o_tail_ref[0]], axis=0), o)
    mix = _dot(o * gate_ref[0], w_ref[...])
    y_ref[0] = _layer_norm_rows(DN_ALPHA * x_ref[0] + mix, g_ref[...], b_ref[...])


def _out_projection(h, o, o_tail, gate, w, ln_g, ln_b):
    bsz, length, d = h.shape
    tl = ROW_TILE
    return pl.pallas_call(
        _out_kernel,
        out_shape=jax.ShapeDtypeStruct(h.shape, F32),
        grid=(bsz, length // tl),
        in_specs=[_row_spec(tl, d), _pair_row_spec(tl),
                  pl.BlockSpec((1,) + o_tail.shape[1:], lambda b, t: (b, 0, 0)),
                  _row_spec(tl, d), _const_spec(w.shape),
                  _const_spec(ln_g.shape), _const_spec(ln_b.shape)],
        out_specs=_row_spec(tl, d),
        compiler_params=_params(),
        name="attn_out_projection",
    )(h, o, o_tail, gate, w, ln_g, ln_b)


def _gate_weights(w_r, w_i):
    per = RNN_GROUP // LRU_BLOCK
    eye = jnp.eye(per, dtype=w_r.dtype)

    def block_diag(w):
        w = w.reshape(N_RNN_GROUPS, per, LRU_BLOCK, LRU_BLOCK)
        return jnp.einsum('gacd,ab->gacbd', w, eye).reshape(N_RNN_GROUPS, RNN_GROUP, RNN_GROUP)

    return jnp.concatenate([block_diag(w_r), block_diag(w_i)], axis=-1).astype(BF16)


def _row(v):
    return v.reshape(1, -1).astype(F32)


def kernel(x, meta, a_w_in, a_conv_w, a_conv_b, a_w_r, a_b_r, a_w_i, a_b_i, a_lambda, a_w_out, kv_w, kv_f_b, b_w_in, b_w_out, f_w_in, f_conv_w, f_conv_b, f_w_out, ln1_g, ln1_b, ln2_g, ln2_b):
    bsz, seq, d = x.shape
    length = seq + N_META
    assert d == D_MODEL and length % ROW_TILE == 0
    h = jnp.concatenate([jnp.broadcast_to(meta.astype(x.dtype), (bsz, N_META, d)), x], axis=1)

    k = v = c_rows = None
    for layer in range(DEPTH):
        if layer < N_A_LAYERS:
            h = _recurrent_layer(
                h, a_w_in[layer].astype(BF16), a_conv_w[layer], _row(a_conv_b[layer]),
                _gate_weights(a_w_r[layer], a_w_i[layer]), _row(a_b_r[layer]), _row(a_b_i[layer]),
                _row(a_lambda[layer]), a_w_out[layer].astype(BF16),
                _row(ln1_g[layer]), _row(ln1_b[layer]))
        else:
            if layer == N_A_LAYERS:
                wf = jnp.pad(kv_w[:, 2 * d:], ((0, 0), (0, LANES - N_FOX_HEADS))).astype(BF16)
                fb = jnp.pad(kv_f_b, (0, LANES - N_FOX_HEADS)).reshape(1, LANES).astype(F32)
                k, v, c_cols = _kv_projection(h, kv_w[:, :d].astype(BF16),
                                              kv_w[:, d:2 * d].astype(BF16), wf, fb)
                n_blk = -(-length // ATT_TILE)
                c_rows = jnp.transpose(c_cols[:, :, :N_FOX_HEADS], (0, 2, 1))
                c_rows = jnp.pad(c_rows, ((0, 0), (0, 0), (0, n_blk * ATT_TILE - length)))
                c_rows = c_rows.reshape(bsz, N_FOX_HEADS, n_blk, 1, ATT_TILE)
            j = layer - N_A_LAYERS
            q, gate = _q_projection(h, b_w_in[j].astype(BF16))
            o, o_tail = _attention(q, k, v, c_rows)
            h = _out_projection(h, o, o_tail, gate, b_w_out[j].astype(BF16),
                                _row(ln1_g[layer]), _row(ln1_b[layer]))
        h = _ffn_layer(h, f_w_in[layer].astype(BF16), f_conv_w[layer], _row(f_conv_b[layer]),
                       f_w_out[layer].astype(BF16), _row(ln2_g[layer]), _row(ln2_b[layer]))
    return h[:, N_META:]
```

```python
import functools
import math

import jax
import jax.numpy as jnp
from jax import lax
from jax.experimental import pallas as pl
from jax.experimental.pallas import tpu as pltpu

D_MODEL = 1024
DEPTH = 4
N_META = 16
N_A_LAYERS = DEPTH // 2
D_RNN = 3 * D_MODEL // 2
N_LRU_BLOCKS = 16
LRU_BLOCK = D_RNN // N_LRU_BLOCKS
LRU_C = 8.0
CONV_A_WIDTH = 4
N_FOX_HEADS = 16
FOX_HEAD_DIM = D_MODEL // N_FOX_HEADS
D_FF = 2816
CONV_F_WIDTH = 3
DN_ALPHA = (2 * DEPTH) ** 0.25
LN_EPS = 1e-5

SUBLANES = 8
LANES = 128
ROW_TILE = 688
RNN_GROUP = 4 * LRU_BLOCK
N_RNN_GROUPS = D_RNN // RNN_GROUP
FF_CHUNK = 256
N_FF_CHUNKS = D_FF // FF_CHUNK
ATT_TILE = 256
ATT_UNROLL = 4
HEADS_PER_STEP = LANES // FOX_HEAD_DIM
MASK_VALUE = -1e30
VMEM_LIMIT = 56 * 1024 * 1024

F32 = jnp.float32
BF16 = jnp.bfloat16


def _dot(a, b):
    return jnp.dot(a, b, preferred_element_type=F32)


def _layer_norm_rows(z, g, b):
    mu = jnp.mean(z, axis=-1, keepdims=True)
    zc = z - mu
    var = jnp.mean(zc * zc, axis=-1, keepdims=True)
    return zc * lax.rsqrt(var + LN_EPS) * g + b


def _gelu_tanh(x):
    c = math.sqrt(2.0 / math.pi)
    return 0.5 * x * (1.0 + jnp.tanh(c * (x + 0.044715 * (x * x * x))))


def _sigmoid(x):
    return 1.0 / (1.0 + jnp.exp2(x * (-math.log2(math.e))))


def _softplus(x):
    return jnp.maximum(x, 0.0) + jnp.log1p(jnp.exp(-jnp.abs(x)))


def _const_spec(shape):
    return pl.BlockSpec(shape, lambda *_: (0,) * len(shape), pipeline_mode=pl.Buffered(1))


def _row_spec(tl, width):
    return pl.BlockSpec((1, tl, width), lambda b, t: (b, t, 0))


def _params():
    return pltpu.CompilerParams(dimension_semantics=("arbitrary", "arbitrary"),
                                vmem_limit_bytes=VMEM_LIMIT)


def _segment_rows(buf, slab, start, seg):
    return buf[slab, pl.ds(start, SUBLANES, stride=seg), :]


def _causal_conv_interleaved(buf, slab, width, seg, w_ref, b_ref, cols):
    first = SUBLANES - (width - 1)
    cat = jnp.concatenate([_segment_rows(buf, slab, first + q, seg)
                           for q in range(seg + width - 1)], axis=0)
    out = b_ref[:, cols]
    for k in range(width):
        out = out + cat[k * SUBLANES:(k + seg) * SUBLANES, :] * w_ref[k:k + 1, cols]
    return out


def _store_time_order(buf, slab, x, seg):
    for p in range(seg):
        buf[slab, pl.ds(p, SUBLANES, stride=seg), :] = x[p * SUBLANES:(p + 1) * SUBLANES, :]


def _scan_segments(a, u, carry, seg):
    width = a.shape[1]
    h = jnp.zeros((SUBLANES, width), F32)
    prod = jnp.ones((SUBLANES, width), F32)
    local, prods = [], []
    for p in range(seg):
        blk = slice(p * SUBLANES, (p + 1) * SUBLANES)
        h = a[blk] * h + u[blk]
        prod = a[blk] * prod
        local.append(h)
        prods.append(prod)
    row = lax.broadcasted_iota(jnp.int32, (SUBLANES, width), 0)
    h_in = jnp.broadcast_to(carry, (SUBLANES, width))
    for _ in range(SUBLANES - 1):
        h_in = jnp.where(row == 0, carry, pltpu.roll(h + prod * h_in, 1, 0))
    states = [local[p] + prods[p] * h_in for p in range(seg)]
    return states, states[-1][SUBLANES - 1:SUBLANES, :]


def _recurrent_kernel(x_ref, w_in_ref, cw_ref, cb_ref, wg_ref, br_ref, bi_ref, lam_ref,
                      w_out_ref, g_ref, b_ref, o_ref,
                      tail_ref, hc_ref, p_buf, h_buf, y_buf):
    tl = x_ref.shape[1]
    seg = tl // SUBLANES
    slabs = RNN_GROUP // LANES

    @pl.when(pl.program_id(1) == 0)
    def _():
        tail_ref[...] = jnp.zeros_like(tail_ref)
        hc_ref[...] = jnp.zeros_like(hc_ref)

    x = x_ref[0]
    xb = x.astype(BF16)
    for g in range(N_RNN_GROUPS):
        c0, c1 = g * RNN_GROUP, (g + 1) * RNN_GROUP
        gate = _dot(xb, w_in_ref[:, c0:c1])
        rec = _dot(xb, w_in_ref[:, D_RNN + c0:D_RNN + c1])
        tail = tail_ref[:, c0:c1]
        tail_ref[:, c0:c1] = rec[tl - SUBLANES:, :]
        cols = []
        for c in range(slabs):
            lanes = slice(c * LANES, (c + 1) * LANES)
            p_buf[g * slabs + c, 0:SUBLANES, :] = tail[:, lanes]
            p_buf[g * slabs + c, SUBLANES:, :] = rec[:, lanes]
            cols.append(_causal_conv_interleaved(
                p_buf, g * slabs + c, CONV_A_WIDTH, seg, cw_ref, cb_ref,
                slice(c0 + c * LANES, c0 + (c + 1) * LANES)))
        rc = jnp.concatenate(cols, axis=1)
        ri = _dot(rc.astype(BF16), wg_ref[g])
        r = _sigmoid(ri[:, :RNN_GROUP] + br_ref[:, c0:c1])
        i = _sigmoid(ri[:, RNN_GROUP:] + bi_ref[:, c0:c1])
        log_a = (-LRU_C) * r * _softplus(-lam_ref[:, c0:c1])
        a = jnp.exp(log_a)
        t = -jnp.tanh(log_a)
        u = jnp.sqrt(2.0 * t / (1.0 + t)) * (i * rc)
        states, hc_ref[:, c0:c1] = _scan_segments(a, u, hc_ref[:, c0:c1], seg)
        h_all = jnp.concatenate(states, axis=0)
        for c in range(slabs):
            _store_time_order(h_buf, g * slabs + c, h_all[:, c * LANES:(c + 1) * LANES], seg)
        h_time = jnp.concatenate([h_buf[g * slabs + c] for c in range(slabs)], axis=1)
        y_buf[:, c0:c1] = (_gelu_tanh(gate) * h_time).astype(BF16)
    mix = _dot(y_buf[...], w_out_ref[...])
    o_ref[0] = _layer_norm_rows(DN_ALPHA * x + mix, g_ref[...], b_ref[...])


def _recurrent_layer(h, w_in, cw, cb, wg, br, bi, lam, w_out, ln_g, ln_b):
    bsz, length, d = h.shape
    tl = ROW_TILE
    return pl.pallas_call(
        _recurrent_kernel,
        out_shape=jax.ShapeDtypeStruct(h.shape, F32),
        grid=(bsz, length // tl),
        in_specs=[_row_spec(tl, d), _const_spec(w_in.shape), _const_spec(cw.shape),
                  _const_spec(cb.shape), _const_spec(wg.shape), _const_spec(br.shape),
                  _const_spec(bi.shape), _const_spec(lam.shape), _const_spec(w_out.shape),
                  _const_spec(ln_g.shape), _const_spec(ln_b.shape)],
        out_specs=_row_spec(tl, d),
        scratch_shapes=[pltpu.VMEM((SUBLANES, D_RNN), F32),
                        pltpu.VMEM((1, D_RNN), F32),
                        pltpu.VMEM((D_RNN // LANES, tl + SUBLANES, LANES), F32),
                        pltpu.VMEM((D_RNN // LANES, tl, LANES), F32),
                        pltpu.VMEM((tl, D_RNN), BF16)],
        compiler_params=_params(),
        name="recurrent_mixer",
    )(h, w_in, cw, cb, wg, br, bi, lam, w_out, ln_g, ln_b)


def _ffn_kernel(x_ref, w_in_ref, cw_ref, cb_ref, w_out_ref, g_ref, b_ref, o_ref,
                tail_ref, p_buf, y_buf):
    tl = x_ref.shape[1]

    @pl.when(pl.program_id(1) == 0)
    def _():
        tail_ref[...] = jnp.zeros_like(tail_ref)

    x = x_ref[0]
    xb = x.astype(BF16)

    def conv_branch(c0, c1):
        hid = _dot(xb, w_in_ref[:, c0:c1])
        p_buf[0:SUBLANES, :] = tail_ref[:, c0:c1]
        p_buf[SUBLANES:, :] = hid
        tail_ref[:, c0:c1] = hid[tl - SUBLANES:, :]
        out = cb_ref[:, c0:c1]
        for k in range(CONV_F_WIDTH):
            off = SUBLANES - (CONV_F_WIDTH - 1) + k
            out = out + p_buf[off:off + tl, :] * cw_ref[k:k + 1, c0:c1]
        return out

    for c in range(N_FF_CHUNKS):
        c0, c1 = c * FF_CHUNK, (c + 1) * FF_CHUNK
        gate = conv_branch(c0, c1)
        val = conv_branch(D_FF + c0, D_FF + c1)
        y_buf[:, c0:c1] = (_gelu_tanh(gate) * val).astype(BF16)
    ffn = _dot(y_buf[...], w_out_ref[...])
    o_ref[0] = _layer_norm_rows(DN_ALPHA * x + ffn, g_ref[...], b_ref[...])


def _ffn_layer(h, w_in, cw, cb, w_out, ln_g, ln_b):
    bsz, length, d = h.shape
    tl = ROW_TILE
    return pl.pallas_call(
        _ffn_kernel,
        out_shape=jax.ShapeDtypeStruct(h.shape, F32),
        grid=(bsz, length // tl),
        in_specs=[_row_spec(tl, d), _const_spec(w_in.shape), _const_spec(cw.shape),
                  _const_spec(cb.shape), _const_spec(w_out.shape),
                  _const_spec(ln_g.shape), _const_spec(ln_b.shape)],
        out_specs=_row_spec(tl, d),
        scratch_shapes=[pltpu.VMEM((SUBLANES, 2 * D_FF), F32),
                        pltpu.VMEM((tl + SUBLANES, FF_CHUNK), F32),
                        pltpu.VMEM((tl, D_FF), BF16)],
        compiler_params=_params(),
        name="conv_ffn",
    )(h, w_in, cw, cb, w_out, ln_g, ln_b)


def _pair_shape(bsz, length):
    return (bsz, N_FOX_HEADS // HEADS_PER_STEP, length, LANES)


def _pair_row_spec(tl):
    return pl.BlockSpec((1, N_FOX_HEADS // HEADS_PER_STEP, tl, LANES), lambda b, t: (b, 0, t, 0))


def _store_head_pairs(ref, x):
    for hp in range(ref.shape[1]):
        ref[0, hp] = x[:, hp * LANES:(hp + 1) * LANES]


def _load_head_pairs(ref, rows=slice(None)):
    return jnp.concatenate([ref[0, hp, rows, :] for hp in range(ref.shape[1])], axis=1)


def _kv_kernel(x_ref, wk_ref, wv_ref, wf_ref, fb_ref, k_ref, v_ref, c_ref, carry_ref):
    tl = x_ref.shape[1]

    @pl.when(pl.program_id(1) == 0)
    def _():
        carry_ref[...] = jnp.zeros_like(carry_ref)

    xb = x_ref[0].astype(BF16)
    _store_head_pairs(k_ref, _dot(xb, wk_ref[...]).astype(BF16))
    _store_head_pairs(v_ref, _dot(xb, wv_ref[...]).astype(BF16))
    zf = _dot(xb, wf_ref[...]) + fb_ref[...]
    c = jnp.minimum(zf, 0.0) - jnp.log1p(jnp.exp(-jnp.abs(zf)))
    row = lax.broadcasted_iota(jnp.int32, c.shape, 0)
    d = 1
    while d < tl:
        c = c + jnp.where(row >= d, pltpu.roll(c, d, 0), 0.0)
        d *= 2
    c = c + carry_ref[...]
    carry_ref[...] = c[tl - 1:tl, :]
    c_ref[0] = c


def _kv_projection(h, wk, wv, wf, fb):
    bsz, length, d = h.shape
    tl = ROW_TILE
    return pl.pallas_call(
        _kv_kernel,
        out_shape=(jax.ShapeDtypeStruct(_pair_shape(bsz, length), BF16),
                   jax.ShapeDtypeStruct(_pair_shape(bsz, length), BF16),
                   jax.ShapeDtypeStruct((bsz, length, LANES), F32)),
        grid=(bsz, length // tl),
        in_specs=[_row_spec(tl, d), _const_spec(wk.shape), _const_spec(wv.shape),
                  _const_spec(wf.shape), _const_spec(fb.shape)],
        out_specs=(_pair_row_spec(tl), _pair_row_spec(tl), _row_spec(tl, LANES)),
        scratch_shapes=[pltpu.VMEM((1, LANES), F32)],
        compiler_params=_params(),
        name="kv_projection",
    )(h, wk, wv, wf, fb)


def _q_kernel(x_ref, w_ref, q_ref, gate_ref):
    xb = x_ref[0].astype(BF16)
    scale = FOX_HEAD_DIM ** -0.5
    _store_head_pairs(q_ref, (_dot(xb, w_ref[:, :D_MODEL]) * scale).astype(BF16))
    gate_ref[0] = _sigmoid(_dot(xb, w_ref[:, D_MODEL:])).astype(BF16)


def _q_projection(h, w):
    bsz, length, d = h.shape
    tl = ROW_TILE
    return pl.pallas_call(
        _q_kernel,
        out_shape=(jax.ShapeDtypeStruct(_pair_shape(bsz, length), BF16),
                   jax.ShapeDtypeStruct(h.shape, BF16)),
        grid=(bsz, length // tl),
        in_specs=[_row_spec(tl, d), _const_spec(w.shape)],
        out_specs=(_pair_row_spec(tl), _row_spec(tl, d)),
        compiler_params=_params(),
        name="q_projection",
    )(h, w)


def _nt_dot(a, b):
    return lax.dot_general(a, b, (((1,), (1,)), ((), ())), preferred_element_type=F32)


def _attention_kernel(q_ref, k_ref, v_ref, c_ref, o_ref, *scratch):
    t = ATT_TILE
    u = ATT_UNROLL
    s_bufs, p_bufs, a_bufs = scratch[0:u], scratch[u:2 * u], scratch[2 * u:3 * u]
    m_buf, acc_buf, bias_buf, crep_buf, vt_buf, qt_buf = scratch[3 * u:]
    n_tiles = q_ref.shape[0] // t
    n_pairs = n_tiles * (n_tiles + 1) // 2
    assert n_pairs % u == 0
    ones_rows = vt_buf.shape[1] - LANES

    key = lax.broadcasted_iota(jnp.int32, (t, LANES), 0)
    for half in range(t // LANES):
        qry = lax.broadcasted_iota(jnp.int32, (t, LANES), 1) + half * LANES
        bias_buf[0, half] = jnp.zeros((t, LANES), F32)
        bias_buf[1, half] = jnp.where(key <= qry, 0.0, MASK_VALUE)
    head0_rows = lax.broadcasted_iota(jnp.int32, (LANES, 1), 0) < FOX_HEAD_DIM
    for jb in range(n_tiles):
        rows = slice(jb * t, (jb + 1) * t)
        q_t = q_ref[rows, :].astype(F32).T
        qt_buf[jb] = jnp.concatenate([jnp.where(head0_rows, q_t, 0.0),
                                      jnp.where(head0_rows, 0.0, q_t)], axis=1).astype(BF16)
        vt_buf[jb, 0:LANES, :] = v_ref[rows, :].astype(F32).T.astype(BF16)
        vt_buf[jb, LANES:, :] = jnp.ones((ones_rows, t), BF16)
        for hh in range(HEADS_PER_STEP):
            crep_buf[hh, rows, :] = jnp.broadcast_to(c_ref[0, hh, jb], (LANES, t)).T
    o_ref[n_tiles * t:, :] = jnp.zeros((o_ref.shape[0] - n_tiles * t, LANES), BF16)
    for buf in (*p_bufs, *a_bufs, acc_buf):
        buf[...] = jnp.zeros_like(buf)
    m_buf[...] = jnp.full_like(m_buf, MASK_VALUE)

    n_slabs = HEADS_PER_STEP * t // LANES

    def scores(qi, j, s_out):
        s = _dot(k_ref[pl.ds(pl.multiple_of(j * t, t), t), :], qt_buf[qi])
        for c in range(n_slabs):
            s_out[c] = s[:, c * LANES:(c + 1) * LANES]

    def softmax(qi, j, s_in, p_out, a_out):
        masked = (j == qi).astype(jnp.int32)
        keys = pl.ds(pl.multiple_of(j * t, t), t)
        first = j == 0
        for hh in range(HEADS_PER_STEP):
            for half in range(t // LANES):
                c = hh * (t // LANES) + half
                cols = slice(c * LANES, (c + 1) * LANES)
                s = s_in[c] - crep_buf[hh, keys, :] + bias_buf[masked, half]
                m_old = jnp.where(first, MASK_VALUE, m_buf[:, cols])
                m_new = jnp.maximum(m_old, jnp.max(s, axis=0, keepdims=True))
                m_buf[:, cols] = m_new
                a_out[:, cols] = jnp.exp(m_old - m_new)
                p_out[c] = jnp.exp(s - m_new).astype(BF16)

    def accumulate(qi, j, p_in, a_in):
        p = jnp.concatenate([p_in[c] for c in range(n_slabs)], axis=1)
        acc_buf[qi] = acc_buf[qi] * a_in[...] + _dot(vt_buf[j], p)

    def next_pair(pair):
        qi, j = pair
        last = j == qi
        return jnp.minimum(jnp.where(last, qi + 1, qi), n_tiles - 1), jnp.where(last, 0, j + 1)

    def following(pair, count):
        out = []
        for _ in range(count):
            pair = next_pair(pair)
            out.append(pair)
        return out

    def body(_, carry):
        done, cur, nxt0 = carry[0:u], carry[u:2 * u], carry[2 * u]
        nxt = [nxt0] + following(nxt0, u - 1)
        for i in range(u):
            accumulate(*done[i], p_bufs[i], a_bufs[i])
        for i in range(u):
            softmax(*cur[i], s_bufs[i], p_bufs[i], a_bufs[i])
        for i in range(u):
            scores(*nxt[i], s_bufs[i])
        return (*cur, *nxt, next_pair(nxt[-1]))

    zero = jnp.int32(0)
    empty = (zero, zero + 1)
    first = [(zero, zero)] + following((zero, zero), u)
    for i in range(u):
        scores(*first[i], s_bufs[i])
    carry = lax.fori_loop(0, n_pairs // u, body, (*([empty] * u), *first))
    for i in range(u):
        accumulate(*carry[i], p_bufs[i], a_bufs[i])

    for qi in range(n_tiles):
        o2 =acc_buf[qi, 0:LANES, :] / acc_buf[qi, LANES:LANES + 1, :]
        o_ref[qi * t:(qi + 1) * t, :] = (
            jnp.where(head0_rows, o2[:, 0:t], o2[:, t:]).T.astype(BF16))


def _attention_tail_kernel(q_ref, k_ref, v_ref, c_ref, o_ref, m_buf, l_buf, acc_buf):
    t = ATT_TILE
    rem = q_ref.shape[2]
    length = k_ref.shape[2]
    n_full = (length - rem) // t
    rows = N_FOX_HEADS * rem
    lane = lax.broadcasted_iota(jnp.int32, (1, D_MODEL), 1)
    head_lanes = [(lane >= h * FOX_HEAD_DIM) & (lane < (h + 1) * FOX_HEAD_DIM)
                  for h in range(N_FOX_HEADS)]
    q = _load_head_pairs(q_ref)
    qs = jnp.concatenate([jnp.where(hl, q, jnp.zeros_like(q)) for hl in head_lanes], axis=0)
    m_buf[...] = jnp.full_like(m_buf, MASK_VALUE)
    l_buf[...] = jnp.zeros_like(l_buf)
    acc_buf[...] = jnp.zeros_like(acc_buf)

    def update(k_blk, v_blk, c_stack, mask):
        s = _nt_dot(qs, k_blk) - c_stack
        if mask is not None:
            s = jnp.where(mask, s, MASK_VALUE)
        m_old = m_buf[...]
        m_new = jnp.maximum(m_old, jnp.max(s, axis=1, keepdims=True))
        alpha = jnp.exp(m_old - m_new)
        p = jnp.exp(s - m_new)
        m_buf[...] = m_new
        l_buf[...] = alpha * l_buf[...] + jnp.sum(p, axis=1, keepdims=True)
        acc_buf[...] = alpha * acc_buf[...] + _dot(p.astype(BF16), v_blk)

    def c_stacked(j, width):
        return jnp.concatenate([jnp.broadcast_to(c_ref[0, h, j][:, :width], (rem, width))
                                for h in range(N_FOX_HEADS)], axis=0)

    def body(j, carry):
        keys = pl.ds(pl.multiple_of(j * t, t), t)
        update(_load_head_pairs(k_ref, keys), _load_head_pairs(v_ref, keys), c_stacked(j, t), None)
        return carry

    lax.fori_loop(0, n_full, body, 0)
    r = lax.broadcasted_iota(jnp.int32, (rows, rem), 0) % rem
    cc = lax.broadcasted_iota(jnp.int32, (rows, rem), 1)
    last_keys = slice(n_full * t, length)
    update(_load_head_pairs(k_ref, last_keys), _load_head_pairs(v_ref, last_keys),
           c_stacked(n_full, rem), cc <= r)
    o_all = acc_buf[...] / l_buf[...]
    o = o_all[0:rem]
    for h in range(1, N_FOX_HEADS):
        o = jnp.where(head_lanes[h], o_all[h * rem:(h + 1) * rem], o)
    o_ref[0] = o.astype(BF16)


def _attention(q, k, v, c_rows):
    bsz, n_hp, length, _ = q.shape
    d = D_MODEL
    t = ATT_TILE
    n_blk = c_rows.shape[2]
    main = (length // t) * t
    rem = length - main
    qcols = HEADS_PER_STEP * t
    col_spec = pl.BlockSpec((None, None, length, LANES), lambda b, hp: (b, hp, 0, 0))
    o = pl.pallas_call(
        _attention_kernel,
        out_shape=jax.ShapeDtypeStruct(q.shape, BF16),
        grid=(bsz, N_FOX_HEADS // HEADS_PER_STEP),
        in_specs=[col_spec, col_spec, col_spec,
                  pl.BlockSpec((1, HEADS_PER_STEP, n_blk, 1, t), lambda b, hp: (b, hp, 0, 0, 0))],
        out_specs=col_spec,
        scratch_shapes=[*[pltpu.VMEM((qcols // LANES, t, LANES), F32)] * ATT_UNROLL,
                        *[pltpu.VMEM((qcols // LANES, t, LANES), BF16)] * ATT_UNROLL,
                        *[pltpu.VMEM((1, qcols), F32)] * ATT_UNROLL,
                        pltpu.VMEM((1, qcols), F32),
                        pltpu.VMEM((main // t, LANES + 2 * SUBLANES, qcols), F32),
                        pltpu.VMEM((2, t // LANES, t, LANES), F32),
                        pltpu.VMEM((HEADS_PER_STEP, main, LANES), F32),
                        pltpu.VMEM((main // t, LANES + 2 * SUBLANES, t), BF16),
                        pltpu.VMEM((main // t, LANES, qcols), BF16)],
        compiler_params=_params(),
        name="fox_attention",
    )(q, k, v, c_rows)
    assert rem and main % rem == 0
    full_spec = pl.BlockSpec((1, n_hp, length, LANES), lambda b: (b, 0, 0, 0))
    o_tail = pl.pallas_call(
        _attention_tail_kernel,
        out_shape=jax.ShapeDtypeStruct((bsz, rem, d), BF16),
        grid=(bsz,),
        in_specs=[pl.BlockSpec((1, n_hp, rem, LANES), lambda b: (b, 0, main // rem, 0)),
                  full_spec, full_spec,
                  pl.BlockSpec((1, N_FOX_HEADS, n_blk, 1, t), lambda b: (b, 0, 0, 0, 0))],
        out_specs=pl.BlockSpec((1, rem, d), lambda b: (b, 0, 0)),
        scratch_shapes=[pltpu.VMEM((N_FOX_HEADS * rem, 1), F32),
                        pltpu.VMEM((N_FOX_HEADS * rem, 1), F32),
                        pltpu.VMEM((N_FOX_HEADS * rem, d), F32)],
        compiler_params=pltpu.CompilerParams(dimension_semantics=("arbitrary",),
                                             vmem_limit_bytes=VMEM_LIMIT),
        name="fox_attention_tail",
    )(q, k, v, c_rows)
    return o, o_tail


def _out_kernel(x_ref, o_ref, o_tail_ref, gate_ref, w_ref, g_ref, b_ref, y_ref):
    tl = x_ref.shape[1]
    rem = o_tail_ref.shape[1]
    o = _load_head_pairs(o_ref)
    is_last = pl.program_id(1) == pl.num_programs(1) - 1
    o = jnp.where(is_last, jnp.concatenate([o[:tl - rem], o_tail_ref[0]], axis=0), o)
    mix = _dot(o * gate_ref[0], w_ref[...])
    y_ref[0] = _layer_norm_rows(DN_ALPHA * x_ref[0] + mix, g_ref[...], b_ref[...])


def _out_projection(h, o, o_tail, gate, w, ln_g, ln_b):
    bsz, length, d = h.shape
    tl = ROW_TILE
    return pl.pallas_call(
        _out_kernel,
        out_shape=jax.ShapeDtypeStruct(h.shape, F32),
        grid=(bsz, length // tl),
        in_specs=[_row_spec(tl, d), _pair_row_spec(tl),
                  pl.BlockSpec((1,) + o_tail.shape[1:], lambda b, t: (b, 0, 0)),
                  _row_spec(tl, d), _const_spec(w.shape),
                  _const_spec(ln_g.shape), _const_spec(ln_b.shape)],
        out_specs=_row_spec(tl, d),
        compiler_params=_params(),
        name="attn_out_projection",
    )(h, o, o_tail, gate, w, ln_g, ln_b)


def _gate_weights(w_r, w_i):
    per = RNN_GROUP // LRU_BLOCK
    eye = jnp.eye(per, dtype=w_r.dtype)

    def block_diag(w):
        w = w.reshape(N_RNN_GROUPS, per, LRU_BLOCK, LRU_BLOCK)
        return jnp.einsum('gacd,ab->gacbd', w, eye).reshape(N_RNN_GROUPS, RNN_GROUP, RNN_GROUP)

    return jnp.concatenate([block_diag(w_r), block_diag(w_i)], axis=-1).astype(BF16)


def _row(v):
    return v.reshape(1, -1).astype(F32)


def kernel(x, meta, a_w_in, a_conv_w, a_conv_b, a_w_r, a_b_r, a_w_i, a_b_i, a_lambda, a_w_out, kv_w, kv_f_b, b_w_in, b_w_out, f_w_in, f_conv_w, f_conv_b, f_w_out, ln1_g, ln1_b, ln2_g, ln2_b):
    bsz, seq, d = x.shape
    length = seq + N_META
    assert d == D_MODEL and length % ROW_TILE == 0
    h = jnp.concatenate([jnp.broadcast_to(meta.astype(x.dtype), (bsz, N_META, d)), x], axis=1)

    k = v = c_rows = None
    for layer in range(DEPTH):
        if layer < N_A_LAYERS:
            h = _recurrent_layer(
                h, a_w_in[layer].astype(BF16), a_conv_w[layer], _row(a_conv_b[layer]),
                _gate_weights(a_w_r[layer], a_w_i[layer]), _row(a_b_r[layer]), _row(a_b_i[layer]),
                _row(a_lambda[layer]), a_w_out[layer].astype(BF16),
                _row(ln1_g[layer]), _row(ln1_b[layer]))
        else:
            if layer == N_A_LAYERS:
                wf = jnp.pad(kv_w[:, 2 * d:], ((0, 0), (0, LANES - N_FOX_HEADS))).astype(BF16)
                fb = jnp.pad(kv_f_b, (0, LANES - N_FOX_HEADS)).reshape(1, LANES).astype(F32)
                k, v, c_cols = _kv_projection(h, kv_w[:, :d].astype(BF16),
                                              kv_w[:, d:2 * d].astype(BF16), wf, fb)
                n_blk = -(-length // ATT_TILE)
                c_rows = jnp.transpose(c_cols[:, :, :N_FOX_HEADS], (0, 2, 1))
                c_rows = jnp.pad(c_rows, ((0, 0), (0, 0), (0, n_blk * ATT_TILE - length)))
                c_rows = c_rows.reshape(bsz, N_FOX_HEADS, n_blk, 1, ATT_TILE)
            j = layer - N_A_LAYERS
            q, gate = _q_projection(h, b_w_in[j].astype(BF16))
            o, o_tail = _attention(q, k, v, c_rows)
            h = _out_projection(h, o, o_tail, gate, b_w_out[j].astype(BF16),
                                _row(ln1_g[layer]), _row(ln1_b[layer]))
        h = _ffn_layer(h, f_w_in[layer].astype(BF16), f_conv_w[layer], _row(f_conv_b[layer]),
                       f_w_out[layer].astype(BF16), _row(ln2_g[layer]), _row(ln2_b[layer]))
    return h[:, N_META:]
```

```python
import functools
import math

import jax
import jax.numpy as jnp
from jax import lax
from jax.experimental import pallas as pl
from jax.experimental.pallas import tpu as pltpu

D_MODEL = 1024
DEPTH = 4
N_META = 16
N_A_LAYERS = DEPTH // 2
D_RNN = 3 * D_MODEL // 2
N_LRU_BLOCKS = 16
LRU_BLOCK = D_RNN // N_LRU_BLOCKS
LRU_C = 8.0
CONV_A_WIDTH = 4
N_FOX_HEADS = 16
FOX_HEAD_DIM = D_MODEL // N_FOX_HEADS
D_FF = 2816
CONV_F_WIDTH = 3
DN_ALPHA = (2 * DEPTH) ** 0.25
LN_EPS = 1e-5

SUBLANES = 8
LANES = 128
ROW_TILE = 688
RNN_GROUP = 4 * LRU_BLOCK
N_RNN_GROUPS = D_RNN // RNN_GROUP
FF_CHUNK = 256
N_FF_CHUNKS = D_FF // FF_CHUNK
ATT_TILE = 256
ATT_UNROLL = 4
HEADS_PER_STEP = LANES // FOX_HEAD_DIM
MASK_VALUE = -1e30
VMEM_LIMIT = 56 * 1024 * 1024

F32 = jnp.float32
BF16 = jnp.bfloat16


def _dot(a, b):
    return jnp.dot(a, b, preferred_element_type=F32)


def _layer_norm_rows(z, g, b):
    mu = jnp.mean(z, axis=-1, keepdims=True)
    zc = z - mu
    var = jnp.mean(zc * zc, axis=-1, keepdims=True)
    return zc * lax.rsqrt(var + LN_EPS) * g + b


def _gelu_tanh(x):
    c = math.sqrt(2.0 / math.pi)
    return 0.5 * x * (1.0 + jnp.tanh(c * (x + 0.044715 * (x * x * x))))


def _sigmoid(x):
    return 1.0 / (1.0 + jnp.exp2(x * (-math.log2(math.e))))


def _softplus(x):
    return jnp.maximum(x, 0.0) + jnp.log1p(jnp.exp(-jnp.abs(x)))


def _const_spec(shape):
    return pl.BlockSpec(shape, lambda *_: (0,) * len(shape), pipeline_mode=pl.Buffered(1))


def _row_spec(tl, width):
    return pl.BlockSpec((1, tl, width), lambda b, t: (b, t, 0))


def _params():
    return pltpu.CompilerParams(dimension_semantics=("arbitrary", "arbitrary"),
                                vmem_limit_bytes=VMEM_LIMIT)


def _segment_rows(buf, slab, start, seg):
    return buf[slab, pl.ds(start, SUBLANES, stride=seg), :]


def _causal_conv_interleaved(buf, slab, width, seg, w_ref, b_ref, cols):
    first = SUBLANES - (width - 1)
    cat = jnp.concatenate([_segment_rows(buf, slab, first + q, seg)
                           for q in range(seg + width - 1)], axis=0)
    out = b_ref[:, cols]
    for k in range(width):
        out = out + cat[k * SUBLANES:(k + seg) * SUBLANES, :] * w_ref[k:k + 1, cols]
    return out


def _store_time_order(buf, slab, x, seg):
    for p in range(seg):
        buf[slab, pl.ds(p, SUBLANES, stride=seg), :] = x[p * SUBLANES:(p + 1) * SUBLANES, :]


def _scan_segments(a, u, carry, seg):
    width = a.shape[1]
    h = jnp.zeros((SUBLANES, width), F32)
    prod = jnp.ones((SUBLANES, width), F32)
    local, prods = [], []
    for p in range(seg):
        blk = slice(p * SUBLANES, (p + 1) * SUBLANES)
        h = a[blk] * h + u[blk]
        prod = a[blk] * prod
        local.append(h)
        prods.append(prod)
    row = lax.broadcasted_iota(jnp.int32, (SUBLANES, width), 0)
    h_in = jnp.broadcast_to(carry, (SUBLANES, width))
    for _ in range(SUBLANES - 1):
        h_in = jnp.where(row == 0, carry, pltpu.roll(h + prod * h_in, 1, 0))
    states = [local[p] + prods[p] * h_in for p in range(seg)]
    return states, states[-1][SUBLANES - 1:SUBLANES, :]


def _recurrent_kernel(x_ref, w_in_ref, cw_ref, cb_ref, wg_ref, br_ref, bi_ref, lam_ref,
                      w_out_ref, g_ref, b_ref, o_ref,
                      tail_ref, hc_ref, p_buf, h_buf, y_buf):
    tl = x_ref.shape[1]
    seg = tl // SUBLANES
    slabs = RNN_GROUP // LANES

    @pl.when(pl.program_id(1) == 0)
    def _():
        tail_ref[...] = jnp.zeros_like(tail_ref)
        hc_ref[...] = jnp.zeros_like(hc_ref)

    x = x_ref[0]
    xb = x.astype(BF16)
    for g in range(N_RNN_GROUPS):
        c0, c1 = g * RNN_GROUP, (g + 1) * RNN_GROUP
        gate = _dot(xb, w_in_ref[:, c0:c1])
        rec = _dot(xb, w_in_ref[:, D_RNN + c0:D_RNN + c1])
        tail = tail_ref[:, c0:c1]
        tail_ref[:, c0:c1] = rec[tl - SUBLANES:, :]
        cols = []
        for c in range(slabs):
            lanes = slice(c * LANES, (c + 1) * LANES)
            p_buf[g * slabs + c, 0:SUBLANES, :] = tail[:, lanes]
            p_buf[g * slabs + c, SUBLANES:, :] = rec[:, lanes]
            cols.append(_causal_conv_interleaved(
                p_buf, g * slabs + c, CONV_A_WIDTH, seg, cw_ref, cb_ref,
                slice(c0 + c * LANES, c0 + (c + 1) * LANES)))
        rc = jnp.concatenate(cols, axis=1)
        ri = _dot(rc.astype(BF16), wg_ref[g])
        r = _sigmoid(ri[:, :RNN_GROUP] + br_ref[:, c0:c1])
        i = _sigmoid(ri[:, RNN_GROUP:] + bi_ref[:, c0:c1])
        log_a = (-LRU_C) * r * _softplus(-lam_ref[:, c0:c1])
        a = jnp.exp(log_a)
        t = -jnp.tanh(log_a)
        u = jnp.sqrt(2.0 * t / (1.0 + t)) * (i * rc)
        states, hc_ref[:, c0:c1] = _scan_segments(a, u, hc_ref[:, c0:c1], seg)
        h_all = jnp.concatenate(states, axis=0)
        for c in range(slabs):
            _store_time_order(h_buf, g * slabs + c, h_all[:, c * LANES:(c + 1) * LANES], seg)
        h_time = jnp.concatenate([h_buf[g * slabs + c] for c in range(slabs)], axis=1)
        y_buf[:, c0:c1] = (_gelu_tanh(gate) * h_time).astype(BF16)
    mix = _dot(y_buf[...], w_out_ref[...])
    o_ref[0] = _layer_norm_rows(DN_ALPHA * x + mix, g_ref[...], b_ref[...])


def _recurrent_layer(h, w_in, cw, cb, wg, br, bi, lam, w_out, ln_g, ln_b):
    bsz, length, d = h.shape
    tl = ROW_TILE
    return pl.pallas_call(
        _recurrent_kernel,
        out_shape=jax.ShapeDtypeStruct(h.shape, F32),
        grid=(bsz, length // tl),
        in_specs=[_row_spec(tl, d), _const_spec(w_in.shape), _const_spec(cw.shape),
                  _const_spec(cb.shape), _const_spec(wg.shape), _const_spec(br.shape),
                  _const_spec(bi.shape), _const_spec(lam.shape), _const_spec(w_out.shape),
                  _const_spec(ln_g.shape), _const_spec(ln_b.shape)],
        out_specs=_row_spec(tl, d),
        scratch_shapes=[pltpu.VMEM((SUBLANES, D_RNN), F32),
                        pltpu.VMEM((1, D_RNN), F32),
                        pltpu.VMEM((D_RNN // LANES, tl + SUBLANES, LANES), F32),
                        pltpu.VMEM((D_RNN // LANES, tl, LANES), F32),
                        pltpu.VMEM((tl, D_RNN), BF16)],
        compiler_params=_params(),
        name="recurrent_mixer",
    )(h, w_in, cw, cb, wg, br, bi, lam, w_out, ln_g, ln_b)


def _attn_out_rows(x, o_ref, o_tail_ref, gate_ref, w_ref, g_ref, b_ref):
    tl = x.shape[0]
    rem = o_tail_ref.shape[1]
    o = _load_head_pairs(o_ref)
    is_last = pl.program_id(1) == pl.num_programs(1) - 1
    o = jnp.where(is_last, jnp.concatenate([o[:tl - rem], o_tail_ref[0]], axis=0), o)
    mix = _dot(o * gate_ref[0], w_ref[...])
    return _layer_norm_rows(DN_ALPHA * x + mix, g_ref[...], b_ref[...])


def _ffn_kernel(x_ref, w_in_ref, cw_ref, cb_ref, w_out_ref, g_ref, b_ref, *rest, after_attention):
    o_ref, tail_ref, p_buf, y_buf = rest[-4:]
    tl = x_ref.shape[1]

    @pl.when(pl.program_id(1) == 0)
    def _():
        tail_ref[...] = jnp.zeros_like(tail_ref)

    x = x_ref[0]
    if after_attention:
        x = _attn_out_rows(x, *rest[:-4])
    xb = x.astype(BF16)

    def conv_branch(c0, c1):
        hid = _dot(xb, w_in_ref[:, c0:c1])
        p_buf[0:SUBLANES, :] = tail_ref[:, c0:c1]
        p_buf[SUBLANES:, :] = hid
        tail_ref[:, c0:c1] = hid[tl - SUBLANES:, :]
        out = cb_ref[:, c0:c1]
        for k in range(CONV_F_WIDTH):
            off = SUBLANES - (CONV_F_WIDTH - 1) + k
            out = out + p_buf[off:off + tl, :] * cw_ref[k:k + 1, c0:c1]
        return out

    for c in range(N_FF_CHUNKS):
        c0, c1 = c * FF_CHUNK, (c + 1) * FF_CHUNK
        gate = conv_branch(c0, c1)
        val = conv_branch(D_FF + c0, D_FF + c1)
        y_buf[:, c0:c1] = (_gelu_tanh(gate) * val).astype(BF16)
    ffn = _dot(y_buf[...], w_out_ref[...])
    o_ref[0] = _layer_norm_rows(DN_ALPHA * x + ffn, g_ref[...], b_ref[...])


def _ffn_layer(h, w_in, cw, cb, w_out, ln_g, ln_b, attention=None):
    bsz, length, d = h.shape
    tl = ROW_TILE
    operands = [h, w_in, cw, cb, w_out, ln_g, ln_b]
    in_specs = [_row_spec(tl, d), _const_spec(w_in.shape), _const_spec(cw.shape),
                _const_spec(cb.shape), _const_spec(w_out.shape),
                _const_spec(ln_g.shape), _const_spec(ln_b.shape)]
    if attention is not None:
        o, o_tail, gate, a_w, a_g, a_b = attention
        operands += [o, o_tail, gate, a_w, a_g, a_b]
        in_specs += [_pair_row_spec(tl),
                     pl.BlockSpec((1,) + o_tail.shape[1:], lambda b, t: (b, 0, 0)),
                     _row_spec(tl, d), _const_spec(a_w.shape),
                     _const_spec(a_g.shape), _const_spec(a_b.shape)]
    return pl.pallas_call(
        functools.partial(_ffn_kernel, after_attention=attention is not None),
        out_shape=jax.ShapeDtypeStruct(h.shape, F32),
        grid=(bsz, length // tl),
        in_specs=in_specs,
        out_specs=_row_spec(tl, d),
        scratch_shapes=[pltpu.VMEM((SUBLANES, 2 * D_FF), F32),
                        pltpu.VMEM((tl + SUBLANES, FF_CHUNK), F32),
                        pltpu.VMEM((tl, D_FF), BF16)],
        compiler_params=_params(),
        name="conv_ffn",
    )(*operands)


def _pair_shape(bsz, length):
    return (bsz, N_FOX_HEADS // HEADS_PER_STEP, length, LANES)


def _pair_row_spec(tl):
    return pl.BlockSpec((1, N_FOX_HEADS // HEADS_PER_STEP, tl, LANES), lambda b, t: (b, 0, t, 0))


def _store_head_pairs(ref, x):
    for hp in range(ref.shape[1]):
        ref[0, hp] = x[:, hp * LANES:(hp + 1) * LANES]


def _load_head_pairs(ref, rows=slice(None)):
    return jnp.concatenate([ref[0, hp, rows, :] for hp in range(ref.shape[1])], axis=1)


def _kv_kernel(x_ref, wk_ref, wv_ref, wf_ref, fb_ref, k_ref, v_ref, c_ref, carry_ref):
    tl = x_ref.shape[1]

    @pl.when(pl.program_id(1) == 0)
    def _():
        carry_ref[...] = jnp.zeros_like(carry_ref)

    xb = x_ref[0].astype(BF16)
    _store_head_pairs(k_ref, _dot(xb, wk_ref[...]).astype(BF16))
    _store_head_pairs(v_ref, _dot(xb, wv_ref[...]).astype(BF16))
    zf = _dot(xb, wf_ref[...]) + fb_ref[...]
    c = jnp.minimum(zf, 0.0) - jnp.log1p(jnp.exp(-jnp.abs(zf)))
    row = lax.broadcasted_iota(jnp.int32, c.shape, 0)
    d = 1
    while d < tl:
        c = c + jnp.where(row >= d, pltpu.roll(c, d, 0), 0.0)
        d *= 2
    c = c + carry_ref[...]
    carry_ref[...] = c[tl - 1:tl, :]
    c_ref[0] = c


def _kv_projection(h, wk, wv, wf, fb):
    bsz, length, d = h.shape
    tl = ROW_TILE
    return pl.pallas_call(
        _kv_kernel,
        out_shape=(jax.ShapeDtypeStruct(_pair_shape(bsz, length), BF16),
                   jax.ShapeDtypeStruct(_pair_shape(bsz, length), BF16),
                   jax.ShapeDtypeStruct((bsz, length, LANES), F32)),
        grid=(bsz, length // tl),
        in_specs=[_row_spec(tl, d), _const_spec(wk.shape), _const_spec(wv.shape),
                  _const_spec(wf.shape), _const_spec(fb.shape)],
        out_specs=(_pair_row_spec(tl), _pair_row_spec(tl), _row_spec(tl, LANES)),
        scratch_shapes=[pltpu.VMEM((1, LANES), F32)],
        compiler_params=_params(),
        name="kv_projection",
    )(h, wk, wv, wf, fb)


def _q_kernel(x_ref, w_ref, q_ref, gate_ref):
    xb = x_ref[0].astype(BF16)
    scale = FOX_HEAD_DIM ** -0.5
    _store_head_pairs(q_ref, (_dot(xb, w_ref[:, :D_MODEL]) * scale).astype(BF16))
    gate_ref[0] = _sigmoid(_dot(xb, w_ref[:, D_MODEL:])).astype(BF16)


def _q_projection(h, w):
    bsz, length, d = h.shape
    tl = ROW_TILE
    return pl.pallas_call(
        _q_kernel,
        out_shape=(jax.ShapeDtypeStruct(_pair_shape(bsz, length), BF16),
                   jax.ShapeDtypeStruct(h.shape, BF16)),
        grid=(bsz, length // tl),
        in_specs=[_row_spec(tl, d), _const_spec(w.shape)],
        out_specs=(_pair_row_spec(tl), _row_spec(tl, d)),
        compiler_params=_params(),
        name="q_projection",
    )(h, w)


def _nt_dot(a, b):
    return lax.dot_general(a, b, (((1,), (1,)), ((), ())), preferred_element_type=F32)


def _attention_kernel(q_ref, k_ref, v_ref, c_ref, o_ref, *scratch):
    t = ATT_TILE
    u = ATT_UNROLL
    s_bufs, p_bufs, a_bufs = scratch[0:u], scratch[u:2 * u], scratch[2 * u:3 * u]
    m_buf, acc_buf, bias_buf, crep_buf, vt_buf, qt_buf = scratch[3 * u:]
    n_tiles = q_ref.shape[0] // t
    n_pairs = n_tiles * (n_tiles + 1) // 2
    assert n_pairs % u == 0
    ones_rows = vt_buf.shape[1] - LANES

    key = lax.broadcasted_iota(jnp.int32, (t, LANES), 0)
    for half in range(t // LANES):
        qry = lax.broadcasted_iota(jnp.int32, (t, LANES), 1) + half * LANES
        bias_buf[0, half] = jnp.zeros((t, LANES), F32)
        bias_buf[1, half] = jnp.where(key <= qry, 0.0, MASK_VALUE)
    head0_rows = lax.broadcasted_iota(jnp.int32, (LANES, 1), 0) < FOX_HEAD_DIM
    for jb in range(n_tiles):
        rows = slice(jb * t, (jb + 1) * t)
        q_t = q_ref[rows, :].astype(F32).T
        qt_buf[jb] = jnp.concatenate([jnp.where(head0_rows, q_t, 0.0),
                                      jnp.where(head0_rows, 0.0, q_t)], axis=1).astype(BF16)
        vt_buf[jb, 0:LANES, :] = v_ref[rows, :].astype(F32).T.astype(BF16)
        vt_buf[jb, LANES:, :] = jnp.ones((ones_rows, t), BF16)
        for hh in range(HEADS_PER_STEP):
            crep_buf[hh, rows, :] = jnp.broadcast_to(c_ref[0, hh, jb], (LANES, t)).T
    o_ref[n_tiles * t:, :] = jnp.zeros((o_ref.shape[0] - n_tiles * t, LANES), BF16)
    for buf in (*p_bufs, *a_bufs, acc_buf):
        buf[...] = jnp.zeros_like(buf)
    m_buf[...] = jnp.full_like(m_buf, MASK_VALUE)

    n_slabs = HEADS_PER_STEP * t // LANES

    def scores(qi, j, s_out):
        s = _dot(k_ref[pl.ds(pl.multiple_of(j * t, t), t), :], qt_buf[qi])
        for c in range(n_slabs):
            s_out[c] = s[:, c * LANES:(c + 1) * LANES]

    def softmax(qi, j, s_in, p_out, a_out):
        masked = (j == qi).astype(jnp.int32)
        keys = pl.ds(pl.multiple_of(j * t, t), t)
        first = j == 0
        for hh in range(HEADS_PER_STEP):
            for half in range(t // LANES):
                c = hh * (t // LANES) + half
                cols = slice(c * LANES, (c + 1) * LANES)
                s = s_in[c] - crep_buf[hh, keys, :] + bias_buf[masked, half]
                m_old = jnp.where(first, MASK_VALUE, m_buf[:, cols])
                m_new = jnp.maximum(m_old, jnp.max(s, axis=0, keepdims=True))
                m_buf[:, cols] = m_new
                a_out[:, cols] = jnp.exp(m_old - m_new)
                p_out[c] = jnp.exp(s - m_new).astype(BF16)

    def accumulate(qi, j, p_in, a_in):
        p = jnp.concatenate([p_in[c] for c in range(n_slabs)], axis=1)
        acc_buf[qi] = acc_buf[qi] * a_in[...] + _dot(vt_buf[j], p)

    def next_pair(pair):
        qi, j = pair
        last = j == qi
        return jnp.minimum(jnp.where(last, qi + 1, qi), n_tiles - 1), jnp.where(last, 0, j + 1)

    def following(pair, count):
        out = []
        for _ in range(count):
            pair = next_pair(pair)
            out.append(pair)
        return out

    def body(_, carry):
        done, cur, nxt0 = carry[0:u], carry[u:2 * u], carry[2 * u]
        nxt = [nxt0] + following(nxt0, u - 1)
        for i in range(u):
            accumulate(*done[i], p_bufs[i], a_bufs[i])
        for i in range(u):
            softmax(*cur[i], s_bufs[i], p_bufs[i], a_bufs[i])
        for i in range(u):
            scores(*nxt[i], s_bufs[i])
        return (*cur, *nxt, next_pair(nxt[-1]))

    zero = jnp.int32(0)
    empty = (zero, zero + 1)
    first = [(zero, zero)] + following((zero, zero), u)
    for i in range(u):
        scores(*first[i], s_bufs[i])
    carry = lax.fori_loop(0, n_pairs // u, body, (*([empty] * u), *first))
    for i in range(u):
        accumulate(*carry[i], p_bufs[i], a_bufs[i])

    for qi in range(n_tiles):
        o2 =acc_buf[qi, 0:LANES, :] / acc_buf[qi, LANES:LANES + 1, :]
        o_ref[qi * t:(qi + 1) * t, :] = (
            jnp.where(head0_rows, o2[:, 0:t], o2[:, t:]).T.astype(BF16))


def _attention_tail_kernel(q_ref, k_ref, v_ref, c_ref, o_ref, m_buf, l_buf, acc_buf):
    t = ATT_TILE
    rem = q_ref.shape[2]
    length = k_ref.shape[2]
    n_full = (length - rem) // t
    rows = N_FOX_HEADS * rem
    lane = lax.broadcasted_iota(jnp.int32, (1, D_MODEL), 1)
    head_lanes = [(lane >= h * FOX_HEAD_DIM) & (lane < (h + 1) * FOX_HEAD_DIM)
                  for h in range(N_FOX_HEADS)]
    q = _load_head_pairs(q_ref)
    qs = jnp.concatenate([jnp.where(hl, q, jnp.zeros_like(q)) for hl in head_lanes], axis=0)
    m_buf[...] = jnp.full_like(m_buf, MASK_VALUE)
    l_buf[...] = jnp.zeros_like(l_buf)
    acc_buf[...] = jnp.zeros_like(acc_buf)

    def update(k_blk, v_blk, c_stack, mask):
        s = _nt_dot(qs, k_blk) - c_stack
        if mask is not None:
            s = jnp.where(mask, s, MASK_VALUE)
        m_old = m_buf[...]
        m_new = jnp.maximum(m_old, jnp.max(s, axis=1, keepdims=True))
        alpha = jnp.exp(m_old - m_new)
        p = jnp.exp(s - m_new)
        m_buf[...] = m_new
        l_buf[...] = alpha * l_buf[...] + jnp.sum(p, axis=1, keepdims=True)
        acc_buf[...] = alpha * acc_buf[...] + _dot(p.astype(BF16), v_blk)

    def c_stacked(j, width):
        return jnp.concatenate([jnp.broadcast_to(c_ref[0, h, j][:, :width], (rem, width))
                                for h in range(N_FOX_HEADS)], axis=0)

    def body(j, carry):
        keys = pl.ds(pl.multiple_of(j * t, t), t)
        update(_load_head_pairs(k_ref, keys), _load_head_pairs(v_ref, keys), c_stacked(j, t), None)
        return carry

    lax.fori_loop(0, n_full, body, 0)
    r = lax.broadcasted_iota(jnp.int32, (rows, rem), 0) % rem
    cc = lax.broadcasted_iota(jnp.int32, (rows, rem), 1)
    last_keys = slice(n_full * t, length)
    update(_load_head_pairs(k_ref, last_keys), _load_head_pairs(v_ref, last_keys),
           c_stacked(n_full, rem), cc <= r)
    o_all = acc_buf[...] / l_buf[...]
    o = o_all[0:rem]
    for h in range(1, N_FOX_HEADS):
        o = jnp.where(head_lanes[h], o_all[h * rem:(h + 1) * rem], o)
    o_ref[0] = o.astype(BF16)


def _attention(q, k, v, c_rows):
    bsz, n_hp, length, _ = q.shape
    d = D_MODEL
    t = ATT_TILE
    n_blk = c_rows.shape[2]
    main = (length // t) * t
    rem = length - main
    qcols = HEADS_PER_STEP * t
    col_spec = pl.BlockSpec((None, None, length, LANES), lambda b, hp: (b, hp, 0, 0))
    o = pl.pallas_call(
        _attention_kernel,
        out_shape=jax.ShapeDtypeStruct(q.shape, BF16),
        grid=(bsz, N_FOX_HEADS // HEADS_PER_STEP),
        in_specs=[col_spec, col_spec, col_spec,
                  pl.BlockSpec((1, HEADS_PER_STEP, n_blk, 1, t), lambda b, hp: (b, hp, 0, 0, 0))],
        out_specs=col_spec,
        scratch_shapes=[*[pltpu.VMEM((qcols // LANES, t, LANES), F32)] * ATT_UNROLL,
                        *[pltpu.VMEM((qcols // LANES, t, LANES), BF16)] * ATT_UNROLL,
                        *[pltpu.VMEM((1, qcols), F32)] * ATT_UNROLL,
                        pltpu.VMEM((1, qcols), F32),
                        pltpu.VMEM((main // t, LANES + 2 * SUBLANES, qcols), F32),
                        pltpu.VMEM((2, t // LANES, t, LANES), F32),
                        pltpu.VMEM((HEADS_PER_STEP, main, LANES), F32),
                        pltpu.VMEM((main // t, LANES + 2 * SUBLANES, t), BF16),
                        pltpu.VMEM((main // t, LANES, qcols), BF16)],
        compiler_params=_params(),
        name="fox_attention",
    )(q, k, v, c_rows)
    assert rem and main % rem == 0
    full_spec = pl.BlockSpec((1, n_hp, length, LANES), lambda b: (b, 0, 0, 0))
    o_tail = pl.pallas_call(
        _attention_tail_kernel,
        out_shape=jax.ShapeDtypeStruct((bsz, rem, d), BF16),
        grid=(bsz,),
        in_specs=[pl.BlockSpec((1, n_hp, rem, LANES), lambda b: (b, 0, main // rem, 0)),
                  full_spec, full_spec,
                  pl.BlockSpec((1, N_FOX_HEADS, n_blk, 1, t), lambda b: (b, 0, 0, 0, 0))],
        out_specs=pl.BlockSpec((1, rem, d), lambda b: (b, 0, 0)),
        scratch_shapes=[pltpu.VMEM((N_FOX_HEADS * rem, 1), F32),
                        pltpu.VMEM((N_FOX_HEADS * rem, 1), F32),
                        pltpu.VMEM((N_FOX_HEADS * rem, d), F32)],
        compiler_params=pltpu.CompilerParams(dimension_semantics=("arbitrary",),
                                             vmem_limit_bytes=VMEM_LIMIT),
        name="fox_attention_tail",
    )(q, k, v, c_rows)
    return o, o_tail


def _gate_weights(w_r, w_i):
    per = RNN_GROUP // LRU_BLOCK
    eye = jnp.eye(per, dtype=w_r.dtype)

    def block_diag(w):
        w = w.reshape(N_RNN_GROUPS, per, LRU_BLOCK, LRU_BLOCK)
        return jnp.einsum('gacd,ab->gacbd', w, eye).reshape(N_RNN_GROUPS, RNN_GROUP, RNN_GROUP)

    return jnp.concatenate([block_diag(w_r), block_diag(w_i)], axis=-1).astype(BF16)


def _row(v):
    return v.reshape(1, -1).astype(F32)


def kernel(x, meta, a_w_in, a_conv_w, a_conv_b, a_w_r, a_b_r, a_w_i, a_b_i, a_lambda, a_w_out, kv_w, kv_f_b, b_w_in, b_w_out, f_w_in, f_conv_w, f_conv_b, f_w_out, ln1_g, ln1_b, ln2_g, ln2_b):
    bsz, seq, d = x.shape
    length = seq + N_META
    assert d == D_MODEL and length % ROW_TILE == 0
    h = jnp.concatenate([jnp.broadcast_to(meta.astype(x.dtype), (bsz, N_META, d)), x], axis=1)

    k = v = c_rows = None
    for layer in range(DEPTH):
        attention = None
        if layer < N_A_LAYERS:
            h = _recurrent_layer(
                h, a_w_in[layer].astype(BF16), a_conv_w[layer], _row(a_conv_b[layer]),
                _gate_weights(a_w_r[layer], a_w_i[layer]), _row(a_b_r[layer]), _row(a_b_i[layer]),
                _row(a_lambda[layer]), a_w_out[layer].astype(BF16),
                _row(ln1_g[layer]), _row(ln1_b[layer]))
        else:
            if layer == N_A_LAYERS:
                wf = jnp.pad(kv_w[:, 2 * d:], ((0, 0), (0, LANES - N_FOX_HEADS))).astype(BF16)
                fb = jnp.pad(kv_f_b, (0, LANES - N_FOX_HEADS)).reshape(1, LANES).astype(F32)
                k, v, c_cols = _kv_projection(h, kv_w[:, :d].astype(BF16),
                                              kv_w[:, d:2 * d].astype(BF16), wf, fb)
                n_blk = -(-length // ATT_TILE)
                c_rows = jnp.transpose(c_cols[:, :, :N_FOX_HEADS], (0, 2, 1))
                c_rows = jnp.pad(c_rows, ((0, 0), (0, 0), (0, n_blk * ATT_TILE - length)))
                c_rows = c_rows.reshape(bsz, N_FOX_HEADS, n_blk, 1, ATT_TILE)
            j = layer - N_A_LAYERS
            q, gate = _q_projection(h, b_w_in[j].astype(BF16))
            o, o_tail = _attention(q, k, v, c_rows)
            attention = (o, o_tail, gate, b_w_out[j].astype(BF16),
                         _row(ln1_g[layer]), _row(ln1_b[layer]))
        h = _ffn_layer(h, f_w_in[layer].astype(BF16), f_conv_w[layer], _row(f_conv_b[layer]),
                       f_w_out[layer].astype(BF16), _row(ln2_g[layer]), _row(ln2_b[layer]),
                       attention=attention)
    return h[:, N_META:]
```

```python
import functools
import math

import jax
import jax.numpy as jnp
from jax import lax
from jax.experimental import pallas as pl
from jax.experimental.pallas import tpu as pltpu

D_MODEL = 1024
DEPTH = 4
N_META = 16
N_A_LAYERS = DEPTH // 2
D_RNN = 3 * D_MODEL // 2
N_LRU_BLOCKS = 16
LRU_BLOCK = D_RNN // N_LRU_BLOCKS
LRU_C = 8.0
CONV_A_WIDTH = 4
N_FOX_HEADS = 16
FOX_HEAD_DIM = D_MODEL // N_FOX_HEADS
D_FF = 2816
CONV_F_WIDTH = 3
DN_ALPHA = (2 * DEPTH) ** 0.25
LN_EPS = 1e-5

SUBLANES = 8
LANES = 128
ROW_TILE = 688
RNN_GROUP = 4 * LRU_BLOCK
N_RNN_GROUPS = D_RNN // RNN_GROUP
FF_CHUNK = 256
N_FF_CHUNKS = D_FF // FF_CHUNK
ATT_TILE = 256
ATT_UNROLL = 4
HEADS_PER_STEP = LANES // FOX_HEAD_DIM
MASK_VALUE = -1e30
VMEM_LIMIT = 56 * 1024 * 1024

F32 = jnp.float32
BF16 = jnp.bfloat16


def _dot(a, b):
    return jnp.dot(a, b, preferred_element_type=F32)


def _layer_norm_rows(z, g, b):
    mu = jnp.mean(z, axis=-1, keepdims=True)
    zc = z - mu
    var = jnp.mean(zc * zc, axis=-1, keepdims=True)
    return zc * lax.rsqrt(var + LN_EPS) * g + b


def _gelu_tanh(x):
    c = math.sqrt(2.0 / math.pi)
    return 0.5 * x * (1.0 + jnp.tanh(c * (x + 0.044715 * (x * x * x))))


def _sigmoid(x):
    return 1.0 / (1.0 + jnp.exp2(x * (-math.log2(math.e))))


def _softplus(x):
    return jnp.maximum(x, 0.0) + jnp.log1p(jnp.exp(-jnp.abs(x)))


def _const_spec(shape):
    return pl.BlockSpec(shape, lambda *_: (0,) * len(shape), pipeline_mode=pl.Buffered(1))


def _row_spec(tl, width):
    return pl.BlockSpec((1, tl, width), lambda b, t: (b, t, 0))


def _params():
    return pltpu.CompilerParams(dimension_semantics=("arbitrary", "arbitrary"),
                                vmem_limit_bytes=VMEM_LIMIT)


def _segment_rows(buf, slab, start, seg):
    return buf[slab, pl.ds(start, SUBLANES, stride=seg), :]


def _causal_conv_interleaved(buf, slab, width, seg, w_ref, b_ref, cols):
    first = SUBLANES - (width - 1)
    cat = jnp.concatenate([_segment_rows(buf, slab, first + q, seg)
                           for q in range(seg + width - 1)], axis=0)
    out = b_ref[:, cols]
    for k in range(width):
        out = out + cat[k * SUBLANES:(k + seg) * SUBLANES, :] * w_ref[k:k + 1, cols]
    return out


def _store_time_order(buf, slab, x, seg):
    for p in range(seg):
        buf[slab, pl.ds(p, SUBLANES, stride=seg), :] = x[p * SUBLANES:(p + 1) * SUBLANES, :]


def _scan_segments(a, u, carry, seg):
    width = a.shape[1]
    h = jnp.zeros((SUBLANES, width), F32)
    prod = jnp.ones((SUBLANES, width), F32)
    local, prods = [], []
    for p in range(seg):
        blk = slice(p * SUBLANES, (p + 1) * SUBLANES)
        h = a[blk] * h + u[blk]
        prod = a[blk] * prod
        local.append(h)
        prods.append(prod)
    row = lax.broadcasted_iota(jnp.int32, (SUBLANES, width), 0)
    h_in = jnp.broadcast_to(carry, (SUBLANES, width))
    for _ in range(SUBLANES - 1):
        h_in = jnp.where(row == 0, carry, pltpu.roll(h + prod * h_in, 1, 0))
    states = [local[p] + prods[p] * h_in for p in range(seg)]
    return states, states[-1][SUBLANES - 1:SUBLANES, :]


def _recurrent_kernel(x_ref, w_in_ref, cw_ref, cb_ref, wg_ref, br_ref, bi_ref, lam_ref,
                      w_out_ref, g_ref, b_ref, o_ref,
                      tail_ref, hc_ref, p_buf, h_buf, y_buf):
    tl = x_ref.shape[1]
    seg = tl // SUBLANES
    slabs = RNN_GROUP // LANES

    @pl.when(pl.program_id(1) == 0)
    def _():
        tail_ref[...] = jnp.zeros_like(tail_ref)
        hc_ref[...] = jnp.zeros_like(hc_ref)

    x = x_ref[0]
    xb = x.astype(BF16)
    for g in range(N_RNN_GROUPS):
        c0, c1 = g * RNN_GROUP, (g + 1) * RNN_GROUP
        rec = _dot(xb, w_in_ref[:, D_RNN + c0:D_RNN + c1])
        tail = tail_ref[:, c0:c1]
        tail_ref[:, c0:c1] = rec[tl - SUBLANES:, :]
        cols = []
        for c in range(slabs):
            lanes = slice(c * LANES, (c + 1) * LANES)
            p_buf[g * slabs + c, 0:SUBLANES, :] = tail[:, lanes]
            p_buf[g * slabs + c, SUBLANES:, :] = rec[:, lanes]
            cols.append(_causal_conv_interleaved(
                p_buf, g * slabs + c, CONV_A_WIDTH, seg, cw_ref, cb_ref,
                slice(c0 + c * LANES, c0 + (c + 1) * LANES)))
        rc = jnp.concatenate(cols, axis=1)
        ri = _dot(rc.astype(BF16), wg_ref[g])
        r = _sigmoid(ri[:, :RNN_GROUP] + br_ref[:, c0:c1])
        i = _sigmoid(ri[:, RNN_GROUP:] + bi_ref[:, c0:c1])
        log_a = (-LRU_C) * r * _softplus(-lam_ref[:, c0:c1])
        a = jnp.exp(log_a)
        t = -jnp.tanh(log_a)
        u = jnp.sqrt(2.0 * t / (1.0 + t)) * (i * rc)
        states, hc_ref[:, c0:c1] = _scan_segments(a, u, hc_ref[:, c0:c1], seg)
        h_all = jnp.concatenate(states, axis=0)
        for c in range(slabs):
            _store_time_order(h_buf, g * slabs + c, h_all[:, c * LANES:(c + 1) * LANES], seg)
        h_time = jnp.concatenate([h_buf[g * slabs + c] for c in range(slabs)], axis=1)
        gate = _dot(xb, w_in_ref[:, c0:c1])
        y_buf[:, c0:c1] = (_gelu_tanh(gate) * h_time).astype(BF16)
    mix = _dot(y_buf[...], w_out_ref[...])
    o_ref[0] = _layer_norm_rows(DN_ALPHA * x + mix, g_ref[...], b_ref[...])


def _recurrent_layer(h, w_in, cw, cb, wg, br, bi, lam, w_out, ln_g, ln_b):
    bsz, length, d = h.shape
    tl = ROW_TILE
    return pl.pallas_call(
        _recurrent_kernel,
        out_shape=jax.ShapeDtypeStruct(h.shape, F32),
        grid=(bsz, length // tl),
        in_specs=[_row_spec(tl, d), _const_spec(w_in.shape), _const_spec(cw.shape),
                  _const_spec(cb.shape), _const_spec(wg.shape), _const_spec(br.shape),
                  _const_spec(bi.shape), _const_spec(lam.shape), _const_spec(w_out.shape),
                  _const_spec(ln_g.shape), _const_spec(ln_b.shape)],
        out_specs=_row_spec(tl, d),
        scratch_shapes=[pltpu.VMEM((SUBLANES, D_RNN), F32),
                        pltpu.VMEM((1, D_RNN), F32),
                        pltpu.VMEM((D_RNN // LANES, tl + SUBLANES, LANES), F32),
                        pltpu.VMEM((D_RNN // LANES, tl, LANES), F32),
                        pltpu.VMEM((tl, D_RNN), BF16)],
        compiler_params=_params(),
        name="recurrent_mixer",
    )(h, w_in, cw, cb, wg, br, bi, lam, w_out, ln_g, ln_b)


def _attn_out_rows(x, o_ref, o_tail_ref, gate_ref, w_ref, g_ref, b_ref):
    tl = x.shape[0]
    rem = o_tail_ref.shape[1]
    o = _load_head_pairs(o_ref)
    is_last = pl.program_id(1) == pl.num_programs(1) - 1
    o = jnp.where(is_last, jnp.concatenate([o[:tl - rem], o_tail_ref[0]], axis=0), o)
    mix = _dot(o * gate_ref[0], w_ref[...])
    return _layer_norm_rows(DN_ALPHA * x + mix, g_ref[...], b_ref[...])


def _ffn_kernel(x_ref, w_in_ref, cw_ref, cb_ref, w_out_ref, g_ref, b_ref, *rest, after_attention):
    o_ref, tail_ref, p_buf, y_buf = rest[-4:]
    tl = x_ref.shape[1]

    @pl.when(pl.program_id(1) == 0)
    def _():
        tail_ref[...] = jnp.zeros_like(tail_ref)

    x = x_ref[0]
    if after_attention:
        x = _attn_out_rows(x, *rest[:-4])
    xb = x.astype(BF16)

    def conv_branch(c0, c1):
        hid = _dot(xb, w_in_ref[:, c0:c1])
        p_buf[0:SUBLANES, :] = tail_ref[:, c0:c1]
        p_buf[SUBLANES:, :] = hid
        tail_ref[:, c0:c1] = hid[tl - SUBLANES:, :]
        out = cb_ref[:, c0:c1]
        for k in range(CONV_F_WIDTH):
            off = SUBLANES - (CONV_F_WIDTH - 1) + k
            out = out + p_buf[off:off + tl, :] * cw_ref[k:k + 1, c0:c1]
        return out

    for c in range(N_FF_CHUNKS):
        c0, c1 = c * FF_CHUNK, (c + 1) * FF_CHUNK
        gate = conv_branch(c0, c1)
        val = conv_branch(D_FF + c0, D_FF + c1)
        y_buf[:, c0:c1] = (_gelu_tanh(gate) * val).astype(BF16)
    ffn = _dot(y_buf[...], w_out_ref[...])
    o_ref[0] = _layer_norm_rows(DN_ALPHA * x + ffn, g_ref[...], b_ref[...])


def _ffn_layer(h, w_in, cw, cb, w_out, ln_g, ln_b, attention=None):
    bsz, length, d = h.shape
    tl = ROW_TILE
    operands = [h, w_in, cw, cb, w_out, ln_g, ln_b]
    in_specs = [_row_spec(tl, d), _const_spec(w_in.shape), _const_spec(cw.shape),
                _const_spec(cb.shape), _const_spec(w_out.shape),
                _const_spec(ln_g.shape), _const_spec(ln_b.shape)]
    if attention is not None:
        o, o_tail, gate, a_w, a_g, a_b = attention
        operands += [o, o_tail, gate, a_w, a_g, a_b]
        in_specs += [_pair_row_spec(tl),
                     pl.BlockSpec((1,) + o_tail.shape[1:], lambda b, t: (b, 0, 0)),
                     _row_spec(tl, d), _const_spec(a_w.shape),
                     _const_spec(a_g.shape), _const_spec(a_b.shape)]
    return pl.pallas_call(
        functools.partial(_ffn_kernel, after_attention=attention is not None),
        out_shape=jax.ShapeDtypeStruct(h.shape, F32),
        grid=(bsz, length // tl),
        in_specs=in_specs,
        out_specs=_row_spec(tl, d),
        scratch_shapes=[pltpu.VMEM((SUBLANES, 2 * D_FF), F32),
                        pltpu.VMEM((tl + SUBLANES, FF_CHUNK), F32),
                        pltpu.VMEM((tl, D_FF), BF16)],
        compiler_params=_params(),
        name="conv_ffn",
    )(*operands)


def _pair_shape(bsz, length):
    return (bsz, N_FOX_HEADS // HEADS_PER_STEP, length, LANES)


def _pair_row_spec(tl):
    return pl.BlockSpec((1, N_FOX_HEADS // HEADS_PER_STEP, tl, LANES), lambda b, t: (b, 0, t, 0))


def _store_head_pairs(ref, x):
    for hp in range(ref.shape[1]):
        ref[0, hp] = x[:, hp * LANES:(hp + 1) * LANES]


def _load_head_pairs(ref, rows=slice(None)):
    return jnp.concatenate([ref[0, hp, rows, :] for hp in range(ref.shape[1])], axis=1)


def _query_rows(xb, w_ref, q_ref, gate_ref):
    scale = FOX_HEAD_DIM ** -0.5
    _store_head_pairs(q_ref, (_dot(xb, w_ref[:, :D_MODEL]) * scale).astype(BF16))
    gate_ref[0] = _sigmoid(_dot(xb, w_ref[:, D_MODEL:])).astype(BF16)


def _kv_kernel(x_ref, wk_ref, wv_ref, wf_ref, fb_ref, wq_ref,
               k_ref, v_ref, c_ref, q_ref, gate_ref, carry_ref):
    tl = x_ref.shape[1]

    @pl.when(pl.program_id(1) == 0)
    def _():
        carry_ref[...] = jnp.zeros_like(carry_ref)

    xb = x_ref[0].astype(BF16)
    _query_rows(xb, wq_ref, q_ref, gate_ref)
    _store_head_pairs(k_ref, _dot(xb, wk_ref[...]).astype(BF16))
    _store_head_pairs(v_ref, _dot(xb, wv_ref[...]).astype(BF16))
    zf = _dot(xb, wf_ref[...]) + fb_ref[...]
    c = jnp.minimum(zf, 0.0) - jnp.log1p(jnp.exp(-jnp.abs(zf)))
    row = lax.broadcasted_iota(jnp.int32, c.shape, 0)
    d = 1
    while d < tl:
        c = c + jnp.where(row >= d, pltpu.roll(c, d, 0), 0.0)
        d *= 2
    c = c + carry_ref[...]
    carry_ref[...] = c[tl - 1:tl, :]
    c_ref[0] = c


def _kvq_projection(h, wk, wv, wf, fb, wq):
    bsz, length, d = h.shape
    tl = ROW_TILE
    pairs = jax.ShapeDtypeStruct(_pair_shape(bsz, length), BF16)
    return pl.pallas_call(
        _kv_kernel,
        out_shape=(pairs, pairs, jax.ShapeDtypeStruct((bsz, length, LANES), F32),
                   pairs, jax.ShapeDtypeStruct(h.shape, BF16)),
        grid=(bsz, length // tl),
        in_specs=[_row_spec(tl, d), _const_spec(wk.shape), _const_spec(wv.shape),
                  _const_spec(wf.shape), _const_spec(fb.shape), _const_spec(wq.shape)],
        out_specs=(_pair_row_spec(tl), _pair_row_spec(tl), _row_spec(tl, LANES),
                   _pair_row_spec(tl), _row_spec(tl, d)),
        scratch_shapes=[pltpu.VMEM((1, LANES), F32)],
        compiler_params=_params(),
        name="kvq_projection",
    )(h, wk, wv, wf, fb, wq)


def _q_kernel(x_ref, w_ref, q_ref, gate_ref):
    _query_rows(x_ref[0].astype(BF16), w_ref, q_ref, gate_ref)


def _q_projection(h, w):
    bsz, length, d = h.shape
    tl = ROW_TILE
    return pl.pallas_call(
        _q_kernel,
        out_shape=(jax.ShapeDtypeStruct(_pair_shape(bsz, length), BF16),
                   jax.ShapeDtypeStruct(h.shape, BF16)),
        grid=(bsz, length // tl),
        in_specs=[_row_spec(tl, d), _const_spec(w.shape)],
        out_specs=(_pair_row_spec(tl), _row_spec(tl, d)),
        compiler_params=_params(),
        name="q_projection",
    )(h, w)


def _nt_dot(a, b):
    return lax.dot_general(a, b, (((1,), (1,)), ((), ())), preferred_element_type=F32)


def _attention_kernel(q_ref, k_ref, v_ref, c_ref, o_ref, *scratch):
    t = ATT_TILE
    u = ATT_UNROLL
    s_bufs, p_bufs, a_bufs = scratch[0:u], scratch[u:2 * u], scratch[2 * u:3 * u]
    m_buf, acc_buf, bias_buf, crep_buf, vt_buf, qt_buf = scratch[3 * u:]
    n_tiles = q_ref.shape[0] // t
    n_pairs = n_tiles * (n_tiles + 1) // 2
    assert n_pairs % u == 0
    ones_rows = vt_buf.shape[1] - LANES

    key = lax.broadcasted_iota(jnp.int32, (t, LANES), 0)
    for half in range(t // LANES):
        qry = lax.broadcasted_iota(jnp.int32, (t, LANES), 1) + half * LANES
        bias_buf[0, half] = jnp.zeros((t, LANES), F32)
        bias_buf[1, half] = jnp.where(key <= qry, 0.0, MASK_VALUE)
    head0_rows = lax.broadcasted_iota(jnp.int32, (LANES, 1), 0) < FOX_HEAD_DIM
    for jb in range(n_tiles):
        rows = slice(jb * t, (jb + 1) * t)
        q_t = q_ref[rows, :].astype(F32).T
        qt_buf[jb] = jnp.concatenate([jnp.where(head0_rows, q_t, 0.0),
                                      jnp.where(head0_rows, 0.0, q_t)], axis=1).astype(BF16)
        vt_buf[jb, 0:LANES, :] = v_ref[rows, :].astype(F32).T.astype(BF16)
        vt_buf[jb, LANES:, :] = jnp.ones((ones_rows, t), BF16)
        for hh in range(HEADS_PER_STEP):
            crep_buf[hh, rows, :] = jnp.broadcast_to(c_ref[0, hh, jb], (LANES, t)).T
    o_ref[n_tiles * t:, :] = jnp.zeros((o_ref.shape[0] - n_tiles * t, LANES), BF16)
    for buf in (*p_bufs, *a_bufs, acc_buf):
        buf[...] = jnp.zeros_like(buf)
    m_buf[...] = jnp.full_like(m_buf, MASK_VALUE)

    n_slabs = HEADS_PER_STEP * t // LANES

    def scores(qi, j, s_out):
        s = _dot(k_ref[pl.ds(pl.multiple_of(j * t, t), t), :], qt_buf[qi])
        for c in range(n_slabs):
            s_out[c] = s[:, c * LANES:(c + 1) * LANES]

    def softmax(qi, j, s_in, p_out, a_out):
        masked = (j == qi).astype(jnp.int32)
        keys = pl.ds(pl.multiple_of(j * t, t), t)
        first = j == 0
        for hh in range(HEADS_PER_STEP):
            for half in range(t // LANES):
                c = hh * (t // LANES) + half
                cols = slice(c * LANES, (c + 1) * LANES)
                s = s_in[c] - crep_buf[hh, keys, :] + bias_buf[masked, half]
                m_old = jnp.where(first, MASK_VALUE, m_buf[:, cols])
                m_new = jnp.maximum(m_old, jnp.max(s, axis=0, keepdims=True))
                m_buf[:, cols] = m_new
                a_out[:, cols] = jnp.exp(m_old - m_new)
                p_out[c] = jnp.exp(s - m_new).astype(BF16)

    def accumulate(qi, j, p_in, a_in):
        p = jnp.concatenate([p_in[c] for c in range(n_slabs)], axis=1)
        acc_buf[qi] = acc_buf[qi] * a_in[...] + _dot(vt_buf[j], p)

    def next_pair(pair):
        qi, j = pair
        last = j == qi
        return jnp.minimum(jnp.where(last, qi + 1, qi), n_tiles - 1), jnp.where(last, 0, j + 1)

    def following(pair, count):
        out = []
        for _ in range(count):
            pair = next_pair(pair)
            out.append(pair)
        return out

    def body(_, carry):
        done, cur, nxt0 = carry[0:u], carry[u:2 * u], carry[2 * u]
        nxt = [nxt0] + following(nxt0, u - 1)
        for i in range(u):
            accumulate(*done[i], p_bufs[i], a_bufs[i])
        for i in range(u):
            softmax(*cur[i], s_bufs[i], p_bufs[i], a_bufs[i])
        for i in range(u):
            scores(*nxt[i], s_bufs[i])
        return (*cur, *nxt, next_pair(nxt[-1]))

    zero = jnp.int32(0)
    empty = (zero, zero + 1)
    first = [(zero, zero)] + following((zero, zero), u)
    for i in range(u):
        scores(*first[i], s_bufs[i])
    carry = lax.fori_loop(0, n_pairs // u, body, (*([empty] * u), *first))
    for i in range(u):
        accumulate(*carry[i], p_bufs[i], a_bufs[i])

    for qi in range(n_tiles):
        o2 =acc_buf[qi, 0:LANES, :] / acc_buf[qi, LANES:LANES + 1, :]
        o_ref[qi * t:(qi + 1) * t, :] = (
            jnp.where(head0_rows, o2[:, 0:t], o2[:, t:]).T.astype(BF16))


def _attention_tail_kernel(q_ref, k_ref, v_ref, c_ref, o_ref, m_buf, l_buf, acc_buf):
    t = ATT_TILE
    rem = q_ref.shape[2]
    length = k_ref.shape[2]
    n_full = (length - rem) // t
    rows = N_FOX_HEADS * rem
    lane = lax.broadcasted_iota(jnp.int32, (1, D_MODEL), 1)
    head_lanes = [(lane >= h * FOX_HEAD_DIM) & (lane < (h + 1) * FOX_HEAD_DIM)
                  for h in range(N_FOX_HEADS)]
    q = _load_head_pairs(q_ref)
    qs = jnp.concatenate([jnp.where(hl, q, jnp.zeros_like(q)) for hl in head_lanes], axis=0)
    m_buf[...] = jnp.full_like(m_buf, MASK_VALUE)
    l_buf[...] = jnp.zeros_like(l_buf)
    acc_buf[...] = jnp.zeros_like(acc_buf)

    def update(k_blk, v_blk, c_stack, mask):
        s = _nt_dot(qs, k_blk) - c_stack
        if mask is not None:
            s = jnp.where(mask, s, MASK_VALUE)
        m_old = m_buf[...]
        m_new = jnp.maximum(m_old, jnp.max(s, axis=1, keepdims=True))
        alpha = jnp.exp(m_old - m_new)
        p = jnp.exp(s - m_new)
        m_buf[...] = m_new
        l_buf[...] = alpha * l_buf[...] + jnp.sum(p, axis=1, keepdims=True)
        acc_buf[...] = alpha * acc_buf[...] + _dot(p.astype(BF16), v_blk)

    def c_stacked(j, width):
        return jnp.concatenate([jnp.broadcast_to(c_ref[0, h, j][:, :width], (rem, width))
                                for h in range(N_FOX_HEADS)], axis=0)

    def body(j, carry):
        keys = pl.ds(pl.multiple_of(j * t, t), t)
        update(_load_head_pairs(k_ref, keys), _load_head_pairs(v_ref, keys), c_stacked(j, t), None)
        return carry

    lax.fori_loop(0, n_full, body, 0)
    r = lax.broadcasted_iota(jnp.int32, (rows, rem), 0) % rem
    cc = lax.broadcasted_iota(jnp.int32, (rows, rem), 1)
    last_keys = slice(n_full * t, length)
    update(_load_head_pairs(k_ref, last_keys), _load_head_pairs(v_ref, last_keys),
           c_stacked(n_full, rem), cc <= r)
    o_all = acc_buf[...] / l_buf[...]
    o = o_all[0:rem]
    for h in range(1, N_FOX_HEADS):
        o = jnp.where(head_lanes[h], o_all[h * rem:(h + 1) * rem], o)
    o_ref[0] = o.astype(BF16)


def _attention(q, k, v, c_rows):
    bsz, n_hp, length, _ = q.shape
    d = D_MODEL
    t = ATT_TILE
    n_blk = c_rows.shape[2]
    main = (length // t) * t
    rem = length - main
    qcols = HEADS_PER_STEP * t
    col_spec = pl.BlockSpec((None, None, length, LANES), lambda b, hp: (b, hp, 0, 0))
    o = pl.pallas_call(
        _attention_kernel,
        out_shape=jax.ShapeDtypeStruct(q.shape, BF16),
        grid=(bsz, N_FOX_HEADS // HEADS_PER_STEP),
        in_specs=[col_spec, col_spec, col_spec,
                  pl.BlockSpec((1, HEADS_PER_STEP, n_blk, 1, t), lambda b, hp: (b, hp, 0, 0, 0))],
        out_specs=col_spec,
        scratch_shapes=[*[pltpu.VMEM((qcols // LANES, t, LANES), F32)] * ATT_UNROLL,
                        *[pltpu.VMEM((qcols // LANES, t, LANES), BF16)] * ATT_UNROLL,
                        *[pltpu.VMEM((1, qcols), F32)] * ATT_UNROLL,
                        pltpu.VMEM((1, qcols), F32),
                        pltpu.VMEM((main // t, LANES + 2 * SUBLANES, qcols), F32),
                        pltpu.VMEM((2, t // LANES, t, LANES), F32),
                        pltpu.VMEM((HEADS_PER_STEP, main, LANES), F32),
                        pltpu.VMEM((main // t, LANES + 2 * SUBLANES, t), BF16),
                        pltpu.VMEM((main // t, LANES, qcols), BF16)],
        compiler_params=_params(),
        name="fox_attention",
    )(q, k, v, c_rows)
    assert rem and main % rem == 0
    full_spec = pl.BlockSpec((1, n_hp, length, LANES), lambda b: (b, 0, 0, 0))
    o_tail = pl.pallas_call(
        _attention_tail_kernel,
        out_shape=jax.ShapeDtypeStruct((bsz, rem, d), BF16),
        grid=(bsz,),
        in_specs=[pl.BlockSpec((1, n_hp, rem, LANES), lambda b: (b, 0, main // rem, 0)),
                  full_spec, full_spec,
                  pl.BlockSpec((1, N_FOX_HEADS, n_blk, 1, t), lambda b: (b, 0, 0, 0, 0))],
        out_specs=pl.BlockSpec((1, rem, d), lambda b: (b, 0, 0)),
        scratch_shapes=[pltpu.VMEM((N_FOX_HEADS * rem, 1), F32),
                        pltpu.VMEM((N_FOX_HEADS * rem, 1), F32),
                        pltpu.VMEM((N_FOX_HEADS * rem, d), F32)],
        compiler_params=pltpu.CompilerParams(dimension_semantics=("arbitrary",),
                                             vmem_limit_bytes=VMEM_LIMIT),
        name="fox_attention_tail",
    )(q, k, v, c_rows)
    return o, o_tail


def _gate_weights(w_r, w_i):
    per = RNN_GROUP // LRU_BLOCK
    eye = jnp.eye(per, dtype=w_r.dtype)

    def block_diag(w):
        w = w.reshape(N_RNN_GROUPS, per, LRU_BLOCK, LRU_BLOCK)
        return jnp.einsum('gacd,ab->gacbd', w, eye).reshape(N_RNN_GROUPS, RNN_GROUP, RNN_GROUP)

    return jnp.concatenate([block_diag(w_r), block_diag(w_i)], axis=-1).astype(BF16)


def _row(v):
    return v.reshape(1, -1).astype(F32)


def kernel(x, meta, a_w_in, a_conv_w, a_conv_b, a_w_r, a_b_r, a_w_i, a_b_i, a_lambda, a_w_out, kv_w, kv_f_b, b_w_in, b_w_out, f_w_in, f_conv_w, f_conv_b, f_w_out, ln1_g, ln1_b, ln2_g, ln2_b):
    bsz, seq, d = x.shape
    length = seq + N_META
    assert d == D_MODEL and length % ROW_TILE == 0
    h = jnp.concatenate([jnp.broadcast_to(meta.astype(x.dtype), (bsz, N_META, d)), x], axis=1)

    k = v = c_rows = q = gate = None
    for layer in range(DEPTH):
        attention = None
        if layer < N_A_LAYERS:
            h = _recurrent_layer(
                h, a_w_in[layer].astype(BF16), a_conv_w[layer], _row(a_conv_b[layer]),
                _gate_weights(a_w_r[layer], a_w_i[layer]), _row(a_b_r[layer]), _row(a_b_i[layer]),
                _row(a_lambda[layer]), a_w_out[layer].astype(BF16),
                _row(ln1_g[layer]), _row(ln1_b[layer]))
        else:
            if layer == N_A_LAYERS:
                wf = jnp.pad(kv_w[:, 2 * d:], ((0, 0), (0, LANES - N_FOX_HEADS))).astype(BF16)
                fb = jnp.pad(kv_f_b, (0, LANES - N_FOX_HEADS)).reshape(1, LANES).astype(F32)
                k, v, c_cols, q, gate = _kvq_projection(
                    h, kv_w[:, :d].astype(BF16), kv_w[:, d:2 * d].astype(BF16), wf, fb,
                    b_w_in[0].astype(BF16))
                n_blk = -(-length // ATT_TILE)
                c_rows = jnp.transpose(c_cols[:, :, :N_FOX_HEADS], (0, 2, 1))
                c_rows = jnp.pad(c_rows, ((0, 0), (0, 0), (0, n_blk * ATT_TILE - length)))
                c_rows = c_rows.reshape(bsz, N_FOX_HEADS, n_blk, 1, ATT_TILE)
            j = layer - N_A_LAYERS
            if j > 0:
                q, gate = _q_projection(h, b_w_in[j].astype(BF16))
            o, o_tail = _attention(q, k, v, c_rows)
            attention = (o, o_tail, gate, b_w_out[j].astype(BF16),
                         _row(ln1_g[layer]), _row(ln1_b[layer]))
        h = _ffn_layer(h, f_w_in[layer].astype(BF16), f_conv_w[layer], _row(f_conv_b[layer]),
                       f_w_out[layer].astype(BF16), _row(ln2_g[layer]), _row(ln2_b[layer]),
                       attention=attention)
    return h[:, N_META:]
```
